```python
import math
import jax
import jax.numpy as jnp
from jax import lax
import numpy as np

D_MODEL = 1024
BATCH = 16
SEQ = 4096
DEPTH = 4

GRID_W = 64
CTX_LEN = 256
N_MIXERS = 2
D_RNN = D_MODEL
RG_HEADS = 4
RG_HEAD_DIM = D_RNN // RG_HEADS
RG_CONV_W = 4
RG_C = 8.0
HY_SHORT_W = 3
HY_EMB_DIM = 33
HY_BANDS = (HY_EMB_DIM - 1) // 2
HY_FILTER_DIM = 64
HY_FAST_DECAY = 0.3
HY_SLOW_DECAY = 1.5
HY_TARGET = 1e-2
D_FF = 2816
FFN_CONV_W = 3
N_MOD = 6
EPS = 1e-6

kernel_name = 'hybrid_rglru_hyena_dit_trunk'


def rms_norm(x, g):
    xf = x.astype(jnp.float32)
    y = xf * lax.rsqrt(jnp.mean(xf * xf, axis=-1, keepdims=True) + EPS)
    return (y * g.astype(jnp.float32)).astype(x.dtype)


def dwconv(x, w, b, pad):
    y = lax.conv_general_dilated(x, w[:, None, :].astype(x.dtype), window_strides=(1,), padding=[pad],
                                 dimension_numbers=('NWC', 'WIO', 'NWC'), feature_group_count=x.shape[-1])
    return y + b.astype(x.dtype)


def grid_order(x, rows, col_major):
    if not col_major:
        return x
    b, n, d = x.shape
    return x.reshape(b, rows, GRID_W, d).transpose(0, 2, 1, 3).reshape(b, n, d)


def raster_order(x, rows, col_major):
    if not col_major:
        return x
    b, n, d = x.shape
    return x.reshape(b, GRID_W, rows, d).transpose(0, 2, 1, 3).reshape(b, n, d)


def linear_scan(a, b, h0, reverse):
    if h0 is not None:
        idx = -1 if reverse else 0
        b = b.at[:, idx].add(a[:, idx] * h0)

    def combine(early, late):
        a1, b1 = early
        a2, b2 = late
        return a1 * a2, a2 * b1 + b2

    _, h = lax.associative_scan(combine, (a, b), reverse=reverse, axis=1)
    return h


def rg_lru(u, w_a, b_a, w_i, b_i, lam, h0, reverse):
    bsz, n, _ = u.shape
    uh = u.reshape(bsz, n, RG_HEADS, RG_HEAD_DIM)
    r = jax.nn.sigmoid((jnp.einsum('bnhi,hij->bnhj', uh, w_a).reshape(bsz, n, D_RNN) + b_a).astype(jnp.float32))
    gi = jax.nn.sigmoid((jnp.einsum('bnhi,hij->bnhj', uh, w_i).reshape(bsz, n, D_RNN) + b_i).astype(jnp.float32))
    log_a = -RG_C * r * jax.nn.softplus(-lam.astype(jnp.float32))
    a = jnp.exp(log_a)
    beta = jnp.sqrt(-jnp.expm1(2.0 * log_a))
    return linear_scan(a, beta * gi * u.astype(jnp.float32), h0, reverse)


def rglru_core(hn, w_in, conv_w, conv_b, w_a, b_a, w_i, b_i, lam, h0_f, h0_b, with_gate):
    if with_gate:
        z = hn @ w_in
        gate, u = z[..., :D_RNN], z[..., D_RNN:]
    else:
        gate, u = None, hn @ w_in[:, D_RNN:]
    u = dwconv(u, conv_w, conv_b, (1, RG_CONV_W - 2))
    hf = rg_lru(u, w_a[0], b_a[0], w_i[0], b_i[0], lam[0], h0_f, False)
    hb = rg_lru(u, w_a[1], b_a[1], w_i[1], b_i[1], lam[1], h0_b, True)
    return gate, hf, hb


def rglru_out(gate, hf, hb, w_out):
    return ((hf + hb).astype(gate.dtype) * jax.nn.gelu(gate)) @ w_out


def hyena_filter(n, pe_w1, pe_b1, pe_w2, pe_b2, pe_w3, pe_b3, pe_w4, freq):
    f32 = jnp.float32
    t = jnp.linspace(0.0, 1.0, n, dtype=f32)[:, None]
    w = (2.0 * math.pi / n) * jnp.arange(n, dtype=f32)[:, None]
    bands = jnp.linspace(1e-4, HY_BANDS - 1, HY_BANDS, dtype=f32)[None, :]
    z = jnp.concatenate([t, jnp.cos(bands * w), -jnp.sin(bands * w)], axis=-1)
    fr = freq.astype(f32)
    hdn = jnp.sin(fr * (z @ pe_w1.astype(f32) + pe_b1.astype(f32)))
    hdn = jnp.sin(fr * (hdn @ pe_w2.astype(f32) + pe_b2.astype(f32)))
    hdn = jnp.sin(fr * (hdn @ pe_w3.astype(f32) + pe_b3.astype(f32)))
    k = hdn @ pe_w4.astype(f32)
    centre = n // 2
    dist = jnp.abs(jnp.arange(n) - centre).astype(f32)[:, None] / centre
    deltas = jnp.linspace(math.log(HY_TARGET) / HY_SLOW_DECAY, math.log(HY_TARGET) / HY_FAST_DECAY, D_MODEL, dtype=f32)
    k = k * jnp.exp(-dist * jnp.abs(deltas)[None, :])
    return k / jnp.sum(jnp.abs(k), axis=0, keepdims=True)


def long_conv_centred(u, k):
    n = u.shape[1]
    nfft = 2 * n
    centre = n // 2
    uf = jnp.fft.rfft(u.astype(jnp.float32), n=nfft, axis=1)
    kf = jnp.fft.rfft(k, n=nfft, axis=0)
    return jnp.fft.irfft(uf * kf[None], n=nfft, axis=1)[:, centre:centre + n]


def hyena_mixer(hn, w_in, short_w, short_b, pe_w1, pe_b1, pe_w2, pe_b2, pe_w3, pe_b3, pe_w4, freq, skip, w_out):
    n = hn.shape[1]
    z = dwconv(hn @ w_in, short_w, short_b, (1, 1))
    x0, x1, v = jnp.split(z, 3, axis=-1)
    xv = x1 * v
    k = hyena_filter(n, pe_w1, pe_b1, pe_w2, pe_b2, pe_w3, pe_b3, pe_w4, freq)
    y = long_conv_centred(xv, k) + xv.astype(jnp.float32) * skip.astype(jnp.float32)
    return (x0 * y.astype(hn.dtype)) @ w_out


def conv_ffn(hn, w_up, conv_w, conv_b, w_down):
    z = dwconv(hn @ w_up, conv_w, conv_b, (1, 1))
    g, u = jnp.split(z, 2, axis=-1)
    return (jax.nn.silu(g) * u) @ w_down


def setup_inputs(seed: int = 0) -> dict:
    key = jax.random.key(seed)
    ks = iter(jax.random.split(key, 48))
    D = D_MODEL
    n_a = len(range(0, DEPTH, N_MIXERS))
    n_b = len(range(1, DEPTH, N_MIXERS))

    def nrm(shape, scale):
        return jax.random.normal(next(ks), shape, jnp.float32) * scale

    u_lam = jax.random.uniform(next(ks), (n_a, 2, D_RNN), jnp.float32, minval=0.9, maxval=0.999)
    a_lam = u_lam ** (1.0 / RG_C)
    return {
        'x': nrm((BATCH, SEQ, D), 1.0),
        'c': nrm((BATCH, D), 1.0),
        'ctx': nrm((BATCH, CTX_LEN, D), 1.0),
        'c_ctx': nrm((D,), 1.0),
        'mod_w': nrm((DEPTH, D, N_MOD * D), 0.5 * D ** -0.5),
        'mod_b': nrm((DEPTH, N_MOD * D), 0.02),
        'norm1_g': 1.0 + nrm((DEPTH, D), 0.05),
        'norm2_g': 1.0 + nrm((DEPTH, D), 0.05),
        'final_g': 1.0 + nrm((D,), 0.05),
        'rg_w_in': nrm((n_a, D, 2 * D_RNN), D ** -0.5),
        'rg_conv_w': nrm((n_a, RG_CONV_W, D_RNN), RG_CONV_W ** -0.5),
        'rg_conv_b': nrm((n_a, D_RNN), 0.02),
        'rg_w_a': nrm((n_a, 2, RG_HEADS, RG_HEAD_DIM, RG_HEAD_DIM), RG_HEAD_DIM ** -0.5),
        'rg_b_a': nrm((n_a, 2, D_RNN), 0.02),
        'rg_w_i': nrm((n_a, 2, RG_HEADS, RG_HEAD_DIM, RG_HEAD_DIM), RG_HEAD_DIM ** -0.5),
        'rg_b_i': nrm((n_a, 2, D_RNN), 0.02),
        'rg_lam': jnp.log(a_lam) - jnp.log1p(-a_lam),
        'rg_w_out': nrm((n_a, D_RNN, D), D_RNN ** -0.5),
        'hy_w_in': nrm((n_b, D, 3 * D), D ** -0.5),
        'hy_short_w': nrm((n_b, HY_SHORT_W, 3 * D), HY_SHORT_W ** -0.5),
        'hy_short_b': nrm((n_b, 3 * D), 0.02),
        'hy_pe_w1': nrm((n_b, HY_EMB_DIM, HY_FILTER_DIM), HY_EMB_DIM ** -0.5),
        'hy_pe_b1': nrm((n_b, HY_FILTER_DIM), 0.1),
        'hy_pe_w2': nrm((n_b, HY_FILTER_DIM, HY_FILTER_DIM), HY_FILTER_DIM ** -0.5),
        'hy_pe_b2': nrm((n_b, HY_FILTER_DIM), 0.1),
        'hy_pe_w3': nrm((n_b, HY_FILTER_DIM, HY_FILTER_DIM), HY_FILTER_DIM ** -0.5),
        'hy_pe_b3': nrm((n_b, HY_FILTER_DIM), 0.1),
        'hy_pe_w4': nrm((n_b, HY_FILTER_DIM, D), HY_FILTER_DIM ** -0.5),
        'hy_freq': 1.0 + nrm((n_b, HY_FILTER_DIM), 0.05),
        'hy_skip': nrm((n_b, D), 1.0),
        'hy_w_out': nrm((n_b, D, D), D ** -0.5),
        'ffn_w_up': nrm((DEPTH, D, 2 * D_FF), D ** -0.5),
        'ffn_conv_w': nrm((DEPTH, FFN_CONV_W, 2 * D_FF), FFN_CONV_W ** -0.5),
        'ffn_conv_b': nrm((DEPTH, 2 * D_FF), 0.02),
        'ffn_w_down': nrm((DEPTH, D_FF, D), D_FF ** -0.5),
    }


def reference(x, c, ctx, c_ctx, mod_w, mod_b, norm1_g, norm2_g, final_g,
              rg_w_in, rg_conv_w, rg_conv_b, rg_w_a, rg_b_a, rg_w_i, rg_b_i, rg_lam, rg_w_out,
              hy_w_in, hy_short_w, hy_short_b, hy_pe_w1, hy_pe_b1, hy_pe_w2, hy_pe_b2, hy_pe_w3, hy_pe_b3,
              hy_pe_w4, hy_freq, hy_skip, hy_w_out,
              ffn_w_up, ffn_conv_w, ffn_conv_b, ffn_w_down):
    ROWS = x.shape[1] // GRID_W
    c_act = jax.nn.silu(c)
    cc_act = jax.nn.silu(c_ctx)
    s = ctx
    ctx_needed = [any((l % N_MIXERS) == 0 for l in range(i + 1, DEPTH)) for i in range(DEPTH)]
    for i in range(DEPTH):
        kind = i % N_MIXERS
        j = i // N_MIXERS
        col_major = (i // N_MIXERS) % 2 == 1
        keep_ctx = ctx_needed[i]
        sh1, sc1, g1, sh2, sc2, g2 = jnp.split((c_act @ mod_w[i] + mod_b[i])[:, None, :], N_MOD, axis=-1)
        h = grid_order(x, ROWS, col_major)
        hn = rms_norm(h, norm1_g[i]) * (1.0 + sc1) + sh1
        if kind == 0 or keep_ctx:
            csh1, csc1, cg1, csh2, csc2, cg2 = jnp.split((cc_act @ mod_w[i] + mod_b[i])[None, None, :], N_MOD, axis=-1)
            sn = rms_norm(s, norm1_g[i]) * (1.0 + csc1) + csh1
        if kind == 0:
            rg = (rg_w_in[j], rg_conv_w[j], rg_conv_b[j], rg_w_a[j], rg_b_a[j], rg_w_i[j], rg_b_i[j], rg_lam[j])
            gate_s, hf_s, hb_s = rglru_core(sn, *rg, None, None, keep_ctx)
            gate, hf, hb = rglru_core(hn, *rg, hf_s[:, -1], hb_s[:, 0], True)
            h = h + g1 * rglru_out(gate, hf, hb, rg_w_out[j])
            if keep_ctx:
                s = s + cg1 * rglru_out(gate_s, hf_s, hb_s, rg_w_out[j])
        else:
            hy = (hy_w_in[j], hy_short_w[j], hy_short_b[j], hy_pe_w1[j], hy_pe_b1[j], hy_pe_w2[j], hy_pe_b2[j],
                  hy_pe_w3[j], hy_pe_b3[j], hy_pe_w4[j], hy_freq[j], hy_skip[j], hy_w_out[j])
            h = h + g1 * hyena_mixer(hn, *hy)
            if keep_ctx:
                s = s + cg1 * hyena_mixer(sn, *hy)
        ffn = (ffn_w_up[i], ffn_conv_w[i], ffn_conv_b[i], ffn_w_down[i])
        h = h + g2 * conv_ffn(rms_norm(h, norm2_g[i]) * (1.0 + sc2) + sh2, *ffn)
        if keep_ctx:
            s = s + cg2 * conv_ffn(rms_norm(s, norm2_g[i]) * (1.0 + csc2) + csh2, *ffn)
        x = raster_order(h, ROWS, col_major)
    return rms_norm(x, final_g)
```

```python
import functools
import math

import jax
import jax.numpy as jnp
from jax import lax
from jax.experimental import pallas as pl
from jax.experimental.pallas import tpu as pltpu

F32 = jnp.float32
BF16 = jnp.bfloat16

GRID_W = 64
RG_C = 8.0
HY_FAST_DECAY = 0.3
HY_SLOW_DECAY = 1.5
HY_TARGET = 1e-2
EPS = 1e-6

HALO = 8
LANE = 128
ROW_TILE = 512
RG_TT = 64
RG_BB = 8
DFT_FB = 256
COL_CHUNK = 256
VMEM_LIMIT = 56 * 1024 * 1024


def _pick(n, cap, mult):
    best = None
    for d in range(mult, min(n, cap) + 1, mult):
        if n % d == 0:
            best = d
    return n if best is None else best


def _full(arr, single=True):
    nd = arr.ndim
    if single:
        return pl.BlockSpec(arr.shape, lambda *_: (0,) * nd, pipeline_mode=pl.Buffered(1))
    return pl.BlockSpec(arr.shape, lambda *_: (0,) * nd)


def _params(sem):
    return pltpu.CompilerParams(dimension_semantics=sem, vmem_limit_bytes=VMEM_LIMIT)


def _sigmoid(x):
    return 0.5 + 0.5 * jnp.tanh(0.5 * x)


def _gelu_tanh(x):
    return 0.5 * x * (1.0 + jnp.tanh(math.sqrt(2.0 / math.pi) * (x + 0.044715 * (x * x * x))))


def _norm_mod(x, ng, sc, sh):
    ms = jnp.mean(x * x, axis=-1, keepdims=True)
    y = x * lax.rsqrt(ms + EPS)
    return (y * ng) * (1.0 + sc) + sh


def _shift_rows(z, k):
    if k == 0:
        return z
    return pltpu.roll(z, (-k) % z.shape[0], 0)


def _conv_rows(z, cw, cb, offsets, t):
    acc = None
    for i, o in enumerate(offsets):
        term = cw[i:i + 1, :] * _shift_rows(z, o)[HALO:HALO + t]
        acc = term if acc is None else acc + term
    return acc + cb


def _halo_specs(t_rows, l, d):
    r = t_rows // HALO
    last = l // HALO - 1

    def prev_map(b, t):
        return (b, jnp.maximum(t * r - 1, 0), 0)

    def next_map(b, t):
        return (b, jnp.minimum((t + 1) * r, last), 0)

    return (pl.BlockSpec((1, HALO, d), prev_map),
            pl.BlockSpec((1, t_rows, d), lambda b, t: (b, t, 0)),
            pl.BlockSpec((1, HALO, d), next_map))


def _ext_rows(hp_ref, h_ref, hn_ref, ng, sc, sh):
    t = pl.program_id(1)
    nt = pl.num_programs(1)
    x = h_ref[0]
    xm = _norm_mod(x, ng, sc, sh)
    xp = jnp.where(t > 0, _norm_mod(hp_ref[0], ng, sc, sh), 0.0)
    xn = jnp.where(t < nt - 1, _norm_mod(hn_ref[0], ng, sc, sh), 0.0)
    return x, jnp.concatenate([xp, xm, xn], axis=0).astype(BF16)


def _mod_kernel(c_ref, w_ref, b_ref, o_ref):
    cv = c_ref[...]
    ca = cv * _sigmoid(cv)
    o_ref[0] = jnp.dot(ca.astype(BF16), w_ref[0].astype(BF16), preferred_element_type=F32) + b_ref[0]


def _mod_call(cpad, mod_w, mod_b):
    depth, d, m6 = mod_w.shape
    bp = cpad.shape[0]
    tn = _pick(m6, 1536, LANE)
    return pl.pallas_call(
        _mod_kernel,
        grid=(depth, m6 // tn),
        in_specs=[pl.BlockSpec((bp, d), lambda i, j: (0, 0)),
                  pl.BlockSpec((1, d, tn), lambda i, j: (i, 0, j)),
                  pl.BlockSpec((1, 1, tn), lambda i, j: (i, 0, j))],
        out_specs=pl.BlockSpec((1, bp, tn), lambda i, j: (i, 0, j)),
        out_shape=jax.ShapeDtypeStruct((depth, bp, m6), F32),
        compiler_params=_params(("arbitrary", "arbitrary")),
    )(cpad, mod_w, mod_b.reshape(depth, 1, m6))


def _ffn_kernel(hp_ref, h_ref, hn_ref, sh_ref, sc_ref, g_ref, ng_ref, wup_ref, cw_ref, cb_ref, wdn_ref,
                o_ref, act_s, *, t_rows, f, ck):
    x, xe = _ext_rows(hp_ref, h_ref, hn_ref, ng_ref[...], sc_ref[0], sh_ref[0])
    offs = (-1, 0, 1)
    for ci in range(f // ck):
        lo = ci * ck
        zg = jnp.dot(xe, wup_ref[:, lo:lo + ck], preferred_element_type=F32)
        zu = jnp.dot(xe, wup_ref[:, f + lo:f + lo + ck], preferred_element_type=F32)
        g = _conv_rows(zg, cw_ref[:, lo:lo + ck], cb_ref[:, lo:lo + ck], offs, t_rows)
        u = _conv_rows(zu, cw_ref[:, f + lo:f + lo + ck], cb_ref[:, f + lo:f + lo + ck], offs, t_rows)
        act_s[:, lo:lo + ck] = (g * _sigmoid(g) * u).astype(BF16)
    out = jnp.dot(act_s[...], wdn_ref[...], preferred_element_type=F32)
    o_ref[0] = x + g_ref[0] * out


def _ffn_call(h, sh, sc, g, ng, wup, cw, cb, wdn):
    b, l, d = h.shape
    f = wdn.shape[0]
    t_rows = min(ROW_TILE, l)
    ck = _pick(f, COL_CHUNK, LANE)
    vec = pl.BlockSpec((1, 1, d), lambda bi, ti: (bi, 0, 0))
    kern = functools.partial(_ffn_kernel, t_rows=t_rows, f=f, ck=ck)
    return pl.pallas_call(
        kern,
        grid=(b, l // t_rows),
        in_specs=[*_halo_specs(t_rows, l, d), vec, vec, vec, _full(ng), _full(wup), _full(cw), _full(cb), _full(wdn)],
        out_specs=pl.BlockSpec((1, t_rows, d), lambda bi, ti: (bi, ti, 0)),
        out_shape=jax.ShapeDtypeStruct((b, l, d), F32),
        scratch_shapes=[pltpu.VMEM((t_rows, f), BF16)],
        compiler_params=_params(("parallel", "arbitrary")),
    )(h, h, h, sh, sc, g, ng, wup, cw, cb, wdn)


def _rg_gates(u2, wa_ref, ba_ref, wi_ref, bi_ref, lam_ref, heads):
    d = u2.shape[1]
    hd = d // heads
    ub = u2.astype(BF16)

    def block_diag(w_ref):
        return jnp.concatenate(
            [jnp.dot(ub[:, k * hd:(k + 1) * hd], w_ref[k], preferred_element_type=F32) for k in range(heads)],
            axis=1)

    r = _sigmoid(block_diag(wa_ref) + ba_ref[...])
    gi = _sigmoid(block_diag(wi_ref) + bi_ref[...])
    lam = lam_ref[...]
    softplus_neg = jnp.maximum(-lam, 0.0) + jnp.log1p(jnp.exp(-jnp.abs(lam)))
    log_a = (-RG_C) * r * softplus_neg
    a = jnp.exp(log_a)
    beta = jnp.sqrt(-jnp.tanh(log_a) * (1.0 + a * a))
    return a, beta * gi * u2


def _scan_store(ref, v):
    for c in range(ref.shape[0]):
        ref[c] = v[:, c * LANE:(c + 1) * LANE]


def _scan_load(ref):
    return jnp.concatenate([ref[c] for c in range(ref.shape[0])], axis=1)


def _scan_tile(a_s, b_s, state_s, bb, tt, reverse):
    nc = a_s.shape[0]

    def step(s, hs):
        pos = (tt - 1 - s) if reverse else s
        rows = pl.ds(pos, bb, stride=tt)
        out = []
        for c in range(nc):
            h = a_s[c, rows, :] * hs[c] + b_s[c, rows, :]
            b_s[c, rows, :] = h
            out.append(h)
        return tuple(out)

    h0 = state_s[...]
    hs = lax.fori_loop(0, tt, step, tuple(h0[:, c * LANE:(c + 1) * LANE] for c in range(nc)), unroll=8)
    h = jnp.concatenate(hs, axis=1)
    state_s[...] = h
    return h


def _rg_fwd_kernel(hp_ref, h_ref, hn_ref, sh_ref, sc_ref, ng_ref, win_ref, cw_ref, cb_ref,
                   wa_ref, ba_ref, wi_ref, bi_ref, lam_ref, h0_ref,
                   gate_ref, u_ref, hf_ref, st_ref, a_s, b_s, state_s, *, bb, tt, d, heads):
    t = pl.program_id(1)
    nt = pl.num_programs(1)

    @pl.when(t == 0)
    def _():
        state_s[...] = h0_ref[...]

    ng = ng_ref[...]
    sc = sc_ref[...]
    sh = sh_ref[...]
    te = tt + 2 * HALO
    xm = _norm_mod(h_ref[...], ng, sc, sh)
    xp = jnp.where(t > 0, _norm_mod(hp_ref[...], ng, sc, sh), 0.0)
    xn = jnp.where(t < nt - 1, _norm_mod(hn_ref[...], ng, sc, sh), 0.0)
    xe = jnp.concatenate([xp, xm, xn], axis=1).astype(BF16).reshape(bb * te, d)
    xmb = xm.astype(BF16).reshape(bb * tt, d)

    gate = jnp.dot(xmb, win_ref[:, :d], preferred_element_type=F32)
    gate_ref[...] = gate.reshape(bb, tt, d).astype(gate_ref.dtype)

    ur = jnp.dot(xe, win_ref[:, d:], preferred_element_type=F32)
    cw = cw_ref[...]
    acc = None
    for i, o in enumerate((-1, 0, 1, 2)):
        term = cw[i:i + 1, :] * _shift_rows(ur, o).reshape(bb, te, d)[:, HALO:HALO + tt, :]
        acc = term if acc is None else acc + term
    u = acc + cb_ref[...]
    u_ref[...] = u

    a, b_in = _rg_gates(u.reshape(bb * tt, d), wa_ref, ba_ref, wi_ref, bi_ref, lam_ref, heads)
    _scan_store(a_s, a)
    _scan_store(b_s, b_in)
    st_ref[...] = _scan_tile(a_s, b_s, state_s, bb, tt, reverse=False)
    hf_ref[...] = _scan_load(b_s).reshape(bb, tt, d)


def _rg_bwd_kernel(u_ref, gate_ref, hf_ref, h_ref, g1_ref, wa_ref, ba_ref, wi_ref, bi_ref, lam_ref, h0_ref,
                   wout_ref, o_ref, st_ref, a_s, b_s, state_s, *, bb, tt, d, heads):
    t = pl.program_id(1)

    @pl.when(t == 0)
    def _():
        state_s[...] = h0_ref[...]

    a, b_in = _rg_gates(u_ref[...].reshape(bb * tt, d), wa_ref, ba_ref, wi_ref, bi_ref, lam_ref, heads)
    _scan_store(a_s, a)
    _scan_store(b_s, b_in)
    st_ref[...] = _scan_tile(a_s, b_s, state_s, bb, tt, reverse=True)

    hsum = hf_ref[...].reshape(bb * tt, d) + _scan_load(b_s)
    gate = gate_ref[...].astype(F32).reshape(bb * tt, d)
    y = (hsum * _gelu_tanh(gate)).astype(BF16)
    out = jnp.dot(y, wout_ref[...], preferred_element_type=F32).reshape(bb, tt, d)
    o_ref[...] = h_ref[...] + g1_ref[...] * out


def _rg_layer(h, sh, sc, g1, ng, win, cw, cb, wa, ba, wi, bi, lam, wout, h0_f, h0_b):
    b, l, d = h.shape
    heads = wa.shape[1]
    bb = min(RG_BB, b)
    tt = min(RG_TT, l)
    nt = l // tt
    r = tt // HALO
    grid = (b // bb, nt)
    sem = _params(("parallel", "arbitrary"))

    vec = pl.BlockSpec((bb, 1, d), lambda gi, ti: (gi, 0, 0))
    st_spec = pl.BlockSpec((bb, d), lambda gi, ti: (gi, 0))
    tile_f = pl.BlockSpec((bb, tt, d), lambda gi, ti: (gi, ti, 0))
    prev_spec = pl.BlockSpec((bb, HALO, d), lambda gi, ti: (gi, jnp.maximum(ti * r - 1, 0), 0))
    next_spec = pl.BlockSpec((bb, HALO, d), lambda gi, ti: (gi, jnp.minimum((ti + 1) * r, nt * r - 1), 0))
    scratch = [pltpu.VMEM((d // LANE, bb * tt, LANE), F32), pltpu.VMEM((d // LANE, bb * tt, LANE), F32),
               pltpu.VMEM((bb, d), F32)]

    def dirw(k):
        return wa[k], ba[k], wi[k], bi[k], lam[k]

    fw = dirw(0)
    gate, u, hf, st_f = pl.pallas_call(
        functools.partial(_rg_fwd_kernel, bb=bb, tt=tt, d=d, heads=heads),
        grid=grid,
        in_specs=[prev_spec, tile_f, next_spec, vec, vec, _full(ng), _full(win), _full(cw), _full(cb),
                  *[_full(w) for w in fw], st_spec],
        out_specs=[tile_f, tile_f, tile_f, st_spec],
        out_shape=[jax.ShapeDtypeStruct((b, l, d), BF16), jax.ShapeDtypeStruct((b, l, d), F32),
                   jax.ShapeDtypeStruct((b, l, d), F32), jax.ShapeDtypeStruct((b, d), F32)],
        scratch_shapes=scratch,
        compiler_params=sem,
    )(h, h, h, sh, sc, ng, win, cw, cb, *fw, h0_f)

    tile_r = pl.BlockSpec((bb, tt, d), lambda gi, ti: (gi, nt - 1 - ti, 0))
    bw = dirw(1)
    h_new, st_b = pl.pallas_call(
        functools.partial(_rg_bwd_kernel, bb=bb, tt=tt, d=d, heads=heads),
        grid=grid,
        in_specs=[tile_r, tile_r, tile_r, tile_r, vec, *[_full(w) for w in bw], st_spec, _full(wout)],
        out_specs=[tile_r, st_spec],
        out_shape=[jax.ShapeDtypeStruct((b, l, d), F32), jax.ShapeDtypeStruct((b, d), F32)],
        scratch_shapes=scratch,
        compiler_params=sem,
    )(u, gate, hf, h, g1, *bw, h0_b, wout)
    return h_new, st_f, st_b


def _hy_in_kernel(hp_ref, h_ref, hn_ref, sh_ref, sc_ref, ng_ref, win_ref, cw_ref, cb_ref,
                  x0_ref, xv_ref, *, t_rows, d, ck):
    _, xe = _ext_rows(hp_ref, h_ref, hn_ref, ng_ref[...], sc_ref[0], sh_ref[0])
    offs = (-1, 0, 1)

    def proj(lo):
        z = jnp.dot(xe, win_ref[:, lo:lo + ck], preferred_element_type=F32)
        return _conv_rows(z, cw_ref[:, lo:lo + ck], cb_ref[:, lo:lo + ck], offs, t_rows)

    for ci in range(d // ck):
        lo = ci * ck
        x0_ref[0, :, lo:lo + ck] = proj(lo).astype(BF16)
        xv_ref[0, :, lo:lo + ck] = (proj(d + lo) * proj(2 * d + lo)).astype(BF16)


def _hy_in_call(h, sh, sc, ng, win, cw, cb):
    b, l, d = h.shape
    t_rows = min(ROW_TILE, l)
    ck = _pick(d, COL_CHUNK, LANE)
    vec = pl.BlockSpec((1, 1, d), lambda bi, ti: (bi, 0, 0))
    tile = pl.BlockSpec((1, t_rows, d), lambda bi, ti: (bi, ti, 0))
    return pl.pallas_call(
        functools.partial(_hy_in_kernel, t_rows=t_rows, d=d, ck=ck),
        grid=(b, l // t_rows),
        in_specs=[*_halo_specs(t_rows, l, d), vec, vec, _full(ng), _full(win), _full(cw), _full(cb)],
        out_specs=[tile, tile],
        out_shape=[jax.ShapeDtypeStruct((b, l, d), BF16), jax.ShapeDtypeStruct((b, l, d), BF16)],
        compiler_params=_params(("parallel", "arbitrary")),
    )(h, h, h, sh, sc, ng, win, cw, cb)


def _hy_filter_kernel(w1_ref, b1_ref, w2_ref, b2_ref, w3_ref, b3_ref, w4_ref, fr_ref, k_ref, hdn_s,
                      *, n, d, dc, bands):
    j = pl.program_id(0)
    hp = lax.Precision.HIGHEST

    @pl.when(j == 0)
    def _():
        s = lax.broadcasted_iota(jnp.int32, (n, LANE), 0).astype(F32)
        lane = lax.broadcasted_iota(jnp.int32, (n, LANE), 1)
        tpos = s / float(n - 1)
        w = s * (2.0 * math.pi / n)
        bidx = jnp.where(lane <= bands, lane - 1, lane - 1 - bands).astype(F32)
        band = 1e-4 + bidx * ((bands - 1 - 1e-4) / (bands - 1))
        arg = band * w
        z = jnp.where(lane == 0, tpos,
                      jnp.where(lane <= bands, jnp.cos(arg),
                                jnp.where(lane <= 2 * bands, -jnp.sin(arg), 0.0)))
        fr = fr_ref[...]
        hd = jnp.sin(fr * (jnp.dot(z, w1_ref[...], precision=hp, preferred_element_type=F32) + b1_ref[...]))
        hd = jnp.sin(fr * (jnp.dot(hd, w2_ref[...], precision=hp, preferred_element_type=F32) + b2_ref[...]))
        hd = jnp.sin(fr * (jnp.dot(hd, w3_ref[...], precision=hp, preferred_element_type=F32) + b3_ref[...]))
        hdn_s[...] = hd

    k = jnp.dot(hdn_s[...], w4_ref[...], precision=hp, preferred_element_type=F32)
    srow = lax.broadcasted_iota(jnp.int32, (n, 1), 0).astype(F32)
    centre = n // 2
    dist = jnp.abs(srow - float(centre)) / float(centre)
    ch = (j * dc + lax.broadcasted_iota(jnp.int32, (1, dc), 1)).astype(F32)
    d_lo = math.log(HY_TARGET) / HY_SLOW_DECAY
    d_hi = math.log(HY_TARGET) / HY_FAST_DECAY
    delta = d_lo + ch * ((d_hi - d_lo) / (d - 1))
    k = k * jnp.exp(-dist * jnp.abs(delta))
    k = k / jnp.sum(jnp.abs(k), axis=0, keepdims=True)
    k_ref[...] = k.astype(k_ref.dtype)


def _hy_filter_call(n, w1, b1, w2, b2, w3, b3, w4, fr):
    emb, fd = w1.shape
    d = w4.shape[1]
    bands = (emb - 1) // 2
    w1p = jnp.zeros((LANE, fd), F32).at[:emb].set(w1)
    dc = _pick(d, COL_CHUNK, LANE)
    args = (w1p, b1.reshape(1, fd), w2, b2.reshape(1, fd), w3, b3.reshape(1, fd))
    return pl.pallas_call(
        functools.partial(_hy_filter_kernel, n=n, d=d, dc=dc, bands=bands),
        grid=(d // dc,),
        in_specs=[*[_full(a, single=False) for a in args],
                  pl.BlockSpec((fd, dc), lambda j: (0, j)), _full(fr.reshape(1, fd), single=False)],
        out_specs=pl.BlockSpec((n, dc), lambda j: (0, j)),
        out_shape=jax.ShapeDtypeStruct((n, d), BF16),
        scratch_shapes=[pltpu.VMEM((n, fd), F32)],
        compiler_params=_params(("arbitrary",)),
    )(*args, w4, fr.reshape(1, fd))


def _dft_mats(n):
    i = 2 * jnp.arange(n, dtype=jnp.int32) + 1
    m = (i[:, None] * i[None, :]) % (8 * n)
    ang = m.astype(F32) * (2.0 * math.pi / (8 * n))
    return jnp.cos(ang).astype(BF16), jnp.sin(ang).astype(BF16)


def _hy_spec_kernel(c_ref, s_ref, k_ref, kr_ref, ki_ref, *, n, fb):
    j = pl.program_id(0)
    kk = k_ref[...]
    a = jnp.dot(c_ref[...], kk, preferred_element_type=F32)
    b = jnp.dot(s_ref[...], kk, preferred_element_type=F32)
    fi = 2 * (j * fb + lax.broadcasted_iota(jnp.int32, (fb, 1), 0)) + 1
    m = lax.rem(fi * (n + 1), 8 * n)
    ang = m.astype(F32) * (2.0 * math.pi / (8 * n))
    qr = jnp.cos(ang) * (1.0 / n)
    qi = jnp.sin(ang) * (1.0 / n)
    kr_ref[...] = qr * a + qi * b
    ki_ref[...] = qi * a - qr * b


def _hy_spec_call(cm, sm, k):
    n, d = k.shape
    fb = min(DFT_FB, n)
    row = pl.BlockSpec((fb, n), lambda j: (j, 0))
    out = pl.BlockSpec((fb, d), lambda j: (j, 0))
    return pl.pallas_call(
        functools.partial(_hy_spec_kernel, n=n, fb=fb),
        grid=(n // fb,),
        in_specs=[row, row, _full(k)],
        out_specs=[out, out],
        out_shape=[jax.ShapeDtypeStruct((n, d), F32), jax.ShapeDtypeStruct((n, d), F32)],
        compiler_params=_params(("arbitrary",)),
    )(cm, sm, k)


def _hy_conv_kernel(c_ref, s_ref, xv_ref, kr_ref, ki_ref, x0_ref, h_ref, skip_ref, g1_ref, wout_ref,
                    o_ref, zr_s, zi_s, *, fb):
    p = pl.program_id(1)
    j = pl.program_id(2)
    rows = pl.ds(pl.multiple_of(j * fb, fb), fb)

    @pl.when(p == 0)
    def _():
        xv = xv_ref[0]
        xr = jnp.dot(c_ref[...], xv, preferred_element_type=F32)
        xi = -jnp.dot(s_ref[...], xv, preferred_element_type=F32)
        kr = kr_ref[...]
        ki = ki_ref[...]
        zr_s[rows, :] = (xr * kr - xi * ki).astype(BF16)
        zi_s[rows, :] = (xr * ki + xi * kr).astype(BF16)

    @pl.when(p == 1)
    def _():
        y = (jnp.dot(c_ref[...], zr_s[...], preferred_element_type=F32)
             - jnp.dot(s_ref[...], zi_s[...], preferred_element_type=F32))
        y = y + xv_ref[0, rows, :].astype(F32) * skip_ref[...]
        pv = (x0_ref[0].astype(F32) * y).astype(BF16)
        out = jnp.dot(pv, wout_ref[...], preferred_element_type=F32)
        o_ref[0] = h_ref[0] + g1_ref[0] * out


def _hy_conv_call(cm, sm, xv, kr, ki, x0, h, skip, g1, wout):
    b, n, d = h.shape
    fb = min(DFT_FB, n)
    nf = n // fb
    row = pl.BlockSpec((fb, n), lambda bi, p, j: (j, 0))
    kspec = pl.BlockSpec((fb, d), lambda bi, p, j: (jnp.where(p == 0, j, nf - 1), 0))
    tile = pl.BlockSpec((1, fb, d), lambda bi, p, j: (bi, jnp.where(p == 1, j, 0), 0))
    return pl.pallas_call(
        functools.partial(_hy_conv_kernel, fb=fb),
        grid=(b, 2, nf),
        in_specs=[row, row,
                  pl.BlockSpec((1, n, d), lambda bi, p, j: (bi, 0, 0), pipeline_mode=pl.Buffered(1)),
                  kspec, kspec, tile, tile, _full(skip),
                  pl.BlockSpec((1, 1, d), lambda bi, p, j: (bi, 0, 0)), _full(wout)],
        out_specs=tile,
        out_shape=jax.ShapeDtypeStruct((b, n, d), F32),
        scratch_shapes=[pltpu.VMEM((n, d), BF16), pltpu.VMEM((n, d), BF16)],
        compiler_params=_params(("parallel", "arbitrary", "arbitrary")),
    )(cm, sm, xv, kr, ki, x0, h, skip, g1, wout)


def _to_col_kernel(x_ref, o_ref, *, rows, wb, d):
    for w in range(wb):
        o_ref[0, w * rows:(w + 1) * rows, :] = x_ref[0, :, w * d:(w + 1) * d]


def _to_col_major(x):
    b, l, d = x.shape
    rows = l // GRID_W
    wb = _pick(GRID_W, max(1, ROW_TILE // rows), 1)
    return pl.pallas_call(
        functools.partial(_to_col_kernel, rows=rows, wb=wb, d=d),
        grid=(b, GRID_W // wb),
        in_specs=[pl.BlockSpec((1, rows, wb * d), lambda bi, wi: (bi, 0, wi))],
        out_specs=pl.BlockSpec((1, wb * rows, d), lambda bi, wi: (bi, wi, 0)),
        out_shape=jax.ShapeDtypeStruct((b, l, d), x.dtype),
        compiler_params=_params(("parallel", "arbitrary")),
    )(x.reshape(b, rows, GRID_W * d))


def _final_col_kernel(x_ref, g_ref, o_ref, *, rows, wb, d):
    g = g_ref[...]
    for w in range(wb):
        xw = x_ref[0, w * rows:(w + 1) * rows, :]
        ms = jnp.mean(xw * xw, axis=-1, keepdims=True)
        o_ref[0, :, w * d:(w + 1) * d] = xw * lax.rsqrt(ms + EPS) * g


def _final_row_kernel(x_ref, g_ref, o_ref):
    xw = x_ref[0]
    ms = jnp.mean(xw * xw, axis=-1, keepdims=True)
    o_ref[0] = xw * lax.rsqrt(ms + EPS) * g_ref[...]


def _final_norm(x, g, col_major):
    b, l, d = x.shape
    g2 = g.reshape(1, d)
    if not col_major:
        t_rows = min(ROW_TILE, l)
        tile = pl.BlockSpec((1, t_rows, d), lambda bi, ti: (bi, ti, 0))
        return pl.pallas_call(
            _final_row_kernel, grid=(b, l // t_rows), in_specs=[tile, _full(g2)], out_specs=tile,
            out_shape=jax.ShapeDtypeStruct((b, l, d), F32),
            compiler_params=_params(("parallel", "arbitrary")),
        )(x, g2)
    rows = l // GRID_W
    wb = _pick(GRID_W, max(1, ROW_TILE // rows), 1)
    out = pl.pallas_call(
        functools.partial(_final_col_kernel, rows=rows, wb=wb, d=d),
        grid=(b, GRID_W // wb),
        in_specs=[pl.BlockSpec((1, wb * rows, d), lambda bi, wi: (bi, wi, 0)), _full(g2)],
        out_specs=pl.BlockSpec((1, rows, wb * d), lambda bi, wi: (bi, 0, wi)),
        out_shape=jax.ShapeDtypeStruct((b, rows, GRID_W * d), F32),
        compiler_params=_params(("parallel", "arbitrary")),
    )(x, g2)
    return out.reshape(b, l, d)


def kernel(x, c, ctx, c_ctx, mod_w, mod_b, norm1_g, norm2_g, final_g, rg_w_in, rg_conv_w, rg_conv_b, rg_w_a, rg_b_a, rg_w_i, rg_b_i, rg_lam, rg_w_out, hy_w_in, hy_short_w, hy_short_b, hy_pe_w1, hy_pe_b1, hy_pe_w2, hy_pe_b2, hy_pe_w3, hy_pe_b3, hy_pe_w4, hy_freq, hy_skip, hy_w_out, ffn_w_up, ffn_conv_w, ffn_conv_b, ffn_w_down):
    b, l, d = x.shape
    lc = ctx.shape[1]
    depth = mod_w.shape[0]
    n_mixers = 2
    ctx_needed = [any((q % n_mixers) == 0 for q in range(i + 1, depth)) for i in range(depth)]

    bp = -(-(b + 1) // HALO) * HALO
    cpad = jnp.zeros((bp, d), F32).at[:b].set(c).at[b].set(c_ctx)
    mods = _mod_call(cpad, mod_w, mod_b)

    dft = {}
    h = x
    s = ctx
    col = False
    zeros_state = jnp.zeros((b, d), F32)
    for i in range(depth):
        kind = i % n_mixers
        j = i // n_mixers
        col_major = j % 2 == 1
        keep_ctx = ctx_needed[i]
        if col_major != col:
            assert col_major, "column-major layers are expected to be contiguous at the end"
            h = _to_col_major(h)
            col = True
        lat = [mods[i, :b, q * d:(q + 1) * d].reshape(b, 1, d) for q in range(6)]
        cmod = [jnp.broadcast_to(mods[i, b, q * d:(q + 1) * d].reshape(1, 1, d), (b, 1, d)) for q in range(6)]
        ng1 = norm1_g[i].reshape(1, d)
        ng2 = norm2_g[i].reshape(1, d)

        if kind == 0:
            w = (ng1, rg_w_in[j].astype(BF16), rg_conv_w[j], rg_conv_b[j].reshape(1, d),
                 rg_w_a[j].astype(BF16), rg_b_a[j].reshape(2, 1, d), rg_w_i[j].astype(BF16),
                 rg_b_i[j].reshape(2, 1, d), rg_lam[j].reshape(2, 1, d), rg_w_out[j].astype(BF16))
            s_new, st_f, st_b = _rg_layer(s, cmod[0], cmod[1], cmod[2], *w, zeros_state, zeros_state)
            h, _, _ = _rg_layer(h, lat[0], lat[1], lat[2], *w, st_f, st_b)
            if keep_ctx:
                s = s_new
        else:
            win = hy_w_in[j].astype(BF16)
            cw = hy_short_w[j]
            cb = hy_short_b[j].reshape(1, 3 * d)
            wout = hy_w_out[j].astype(BF16)
            skip = hy_skip[j].reshape(1, d)
            pe = (hy_pe_w1[j], hy_pe_b1[j], hy_pe_w2[j], hy_pe_b2[j], hy_pe_w3[j], hy_pe_b3[j], hy_pe_w4[j],
                  hy_freq[j])
            streams = [(h, lat, True)] + ([(s, cmod, False)] if keep_ctx else [])
            for stream, md, is_lat in streams:
                n = stream.shape[1]
                if n not in dft:
                    dft[n] = _dft_mats(n)
                cm, sm = dft[n]
                kr, ki = _hy_spec_call(cm, sm, _hy_filter_call(n, *pe))
                x0, xv = _hy_in_call(stream, md[0], md[1], ng1, win, cw, cb)
                new = _hy_conv_call(cm, sm, xv, kr, ki, x0, stream, skip, md[2], wout)
                if is_lat:
                    h = new
                else:
                    s = new

        ffn = (ng2, ffn_w_up[i].astype(BF16), ffn_conv_w[i], ffn_conv_b[i].reshape(1, -1), ffn_w_down[i].astype(BF16))
        h = _ffn_call(h, lat[3], lat[4], lat[5], *ffn)
        if keep_ctx:
            s = _ffn_call(s, cmod[3], cmod[4], cmod[5], *ffn)
    return _final_norm(h, final_g, col)
```

```python
import functools
import math

import jax
import jax.numpy as jnp
from jax import lax
from jax.experimental import pallas as pl
from jax.experimental.pallas import tpu as pltpu

F32 = jnp.float32
BF16 = jnp.bfloat16

GRID_W = 64
RG_C = 8.0
HY_FAST_DECAY = 0.3
HY_SLOW_DECAY = 1.5
HY_TARGET = 1e-2
EPS = 1e-6
TINY = 1e-30

SUB = 8
LANE = 128
HALO = SUB
ROW_TILE = 512
RG_TT = 64
DFT_FB = 256
COL_CHUNK = 256
VMEM_LIMIT = 56 * 1024 * 1024


def _pick(n, cap, mult):
    best = None
    for d in range(mult, min(n, cap) + 1, mult):
        if n % d == 0:
            best = d
    return n if best is None else best


def _full(arr, single=True):
    nd = arr.ndim
    if single:
        return pl.BlockSpec(arr.shape, lambda *_: (0,) * nd, pipeline_mode=pl.Buffered(1))
    return pl.BlockSpec(arr.shape, lambda *_: (0,) * nd)


def _params(sem):
    return pltpu.CompilerParams(dimension_semantics=sem, vmem_limit_bytes=VMEM_LIMIT)


def _sigmoid(x):
    return 0.5 + 0.5 * jnp.tanh(0.5 * x)


def _gelu_tanh(x):
    return 0.5 * x * (1.0 + jnp.tanh(math.sqrt(2.0 / math.pi) * (x + 0.044715 * (x * x * x))))


def _norm_mod(x, ng, sc, sh):
    ms = jnp.mean(x * x, axis=-1, keepdims=True)
    y = x * lax.rsqrt(ms + EPS)
    return (y * ng) * (1.0 + sc) + sh


def _shift_rows(z, k):
    if k == 0:
        return z
    return pltpu.roll(z, (-k) % z.shape[0], 0)


def _conv_rows(z, cw, cb, offsets, t):
    acc = None
    for i, o in enumerate(offsets):
        term = cw[i:i + 1, :] * _shift_rows(z, o)[HALO:HALO + t]
        acc = term if acc is None else acc + term
    return acc + cb


def _halo_specs(t_rows, l, d):
    r = t_rows // HALO
    last = l // HALO - 1

    def prev_map(b, t):
        return (b, jnp.maximum(t * r - 1, 0), 0)

    def next_map(b, t):
        return (b, jnp.minimum((t + 1) * r, last), 0)

    return (pl.BlockSpec((1, HALO, d), prev_map),
            pl.BlockSpec((1, t_rows, d), lambda b, t: (b, t, 0)),
            pl.BlockSpec((1, HALO, d), next_map))


def _ext_rows(hp_ref, h_ref, hn_ref, ng, sc, sh):
    t = pl.program_id(1)
    nt = pl.num_programs(1)
    x = h_ref[0]
    xm = _norm_mod(x, ng, sc, sh)
    xp = jnp.where(t > 0, _norm_mod(hp_ref[0], ng, sc, sh), 0.0)
    xn = jnp.where(t < nt - 1, _norm_mod(hn_ref[0], ng, sc, sh), 0.0)
    return x, jnp.concatenate([xp, xm, xn], axis=0).astype(BF16)


def _mod_kernel(c_ref, w_ref, b_ref, o_ref):
    cv = c_ref[...]
    ca = cv * _sigmoid(cv)
    o_ref[0] = jnp.dot(ca.astype(BF16), w_ref[0].astype(BF16), preferred_element_type=F32) + b_ref[0]


def _mod_call(cpad, mod_w, mod_b):
    depth, d, m6 = mod_w.shape
    bp = cpad.shape[0]
    tn = _pick(m6, 1536, LANE)
    return pl.pallas_call(
        _mod_kernel,
        grid=(depth, m6 // tn),
        in_specs=[pl.BlockSpec((bp, d), lambda i, j: (0, 0)),
                  pl.BlockSpec((1, d, tn), lambda i, j: (i, 0, j)),
                  pl.BlockSpec((1, 1, tn), lambda i, j: (i, 0, j))],
        out_specs=pl.BlockSpec((1, bp, tn), lambda i, j: (i, 0, j)),
        out_shape=jax.ShapeDtypeStruct((depth, bp, m6), F32),
        compiler_params=_params(("arbitrary", "arbitrary")),
    )(cpad, mod_w, mod_b.reshape(depth, 1, m6))


def _ffn_kernel(hp_ref, h_ref, hn_ref, sh_ref, sc_ref, g_ref, ng_ref, wup_ref, cw_ref, cb_ref, wdn_ref,
                o_ref, act_s, *, t_rows, f, ck):
    x, xe = _ext_rows(hp_ref, h_ref, hn_ref, ng_ref[...], sc_ref[0], sh_ref[0])
    offs = (-1, 0, 1)
    for ci in range(f // ck):
        lo = ci * ck
        zg = jnp.dot(xe, wup_ref[:, lo:lo + ck], preferred_element_type=F32)
        zu = jnp.dot(xe, wup_ref[:, f + lo:f + lo + ck], preferred_element_type=F32)
        g = _conv_rows(zg, cw_ref[:, lo:lo + ck], cb_ref[:, lo:lo + ck], offs, t_rows)
        u = _conv_rows(zu, cw_ref[:, f + lo:f + lo + ck], cb_ref[:, f + lo:f + lo + ck], offs, t_rows)
        act_s[:, lo:lo + ck] = (g * _sigmoid(g) * u).astype(BF16)
    out = jnp.dot(act_s[...], wdn_ref[...], preferred_element_type=F32)
    o_ref[0] = x + g_ref[0] * out


def _ffn_call(h, sh, sc, g, ng, wup, cw, cb, wdn):
    b, l, d = h.shape
    f = wdn.shape[0]
    t_rows = min(ROW_TILE, l)
    ck = _pick(f, COL_CHUNK, LANE)
    vec = pl.BlockSpec((1, 1, d), lambda bi, ti: (bi, 0, 0))
    kern = functools.partial(_ffn_kernel, t_rows=t_rows, f=f, ck=ck)
    return pl.pallas_call(
        kern,
        grid=(b, l // t_rows),
        in_specs=[*_halo_specs(t_rows, l, d), vec, vec, vec, _full(ng), _full(wup), _full(cw), _full(cb), _full(wdn)],
        out_specs=pl.BlockSpec((1, t_rows, d), lambda bi, ti: (bi, ti, 0)),
        out_shape=jax.ShapeDtypeStruct((b, l, d), F32),
        scratch_shapes=[pltpu.VMEM((t_rows, f), BF16)],
        compiler_params=_params(("parallel", "arbitrary")),
    )(h, h, h, sh, sc, g, ng, wup, cw, cb, wdn)


def _rg_gates(u2, wa_ref, ba_ref, wi_ref, bi_ref, lam_ref, heads):
    d = u2.shape[1]
    hd = d // heads
    ub = u2.astype(BF16)

    def block_diag(w_ref):
        return jnp.concatenate(
            [jnp.dot(ub[:, k * hd:(k + 1) * hd], w_ref[k], preferred_element_type=F32) for k in range(heads)],
            axis=1)

    r = _sigmoid(block_diag(wa_ref) + ba_ref[...])
    gi = _sigmoid(block_diag(wi_ref) + bi_ref[...])
    lam = lam_ref[...]
    softplus_neg = jnp.maximum(-lam, 0.0) + jnp.log1p(jnp.exp(-jnp.abs(lam)))
    log_a = (-RG_C) * r * softplus_neg
    a = jnp.exp(log_a)
    om = -jnp.tanh(log_a) * (1.0 + a * a)
    beta = om * lax.rsqrt(jnp.maximum(om, TINY))
    return a, beta * gi * u2


def _swap_slab_sublane(slabs):
    sub = lax.broadcasted_iota(jnp.int32, slabs[0].shape, 1)
    s = SUB // 2
    while s:
        bit = (sub & s) != 0
        new = list(slabs)
        for k in range(SUB):
            if k & s:
                continue
            lo, hi = slabs[k], slabs[k | s]
            new[k] = jnp.where(bit, pltpu.roll(hi, s, 1), lo)
            new[k | s] = jnp.where(bit, hi, pltpu.roll(lo, SUB - s, 1))
        slabs = new
        s //= 2
    return slabs


def _to_time_major(x):
    _, t, c = x.shape
    slabs = _swap_slab_sublane([x[k].reshape(t // SUB, SUB, c) for k in range(SUB)])
    return jnp.stack(slabs, axis=1).reshape(t * SUB, c)


def _to_batch_major(y, t):
    c = y.shape[1]
    y4 = y.reshape(t // SUB, SUB, SUB, c)
    slabs = _swap_slab_sublane([y4[:, k] for k in range(SUB)])
    return jnp.stack([s.reshape(t, c) for s in slabs], axis=0)


def _scan_tile(a_s, b_s, state_s, tt, reverse):
    def step(s, h):
        pos = (tt - 1 - s) if reverse else s
        rows = pl.ds(pl.multiple_of(pos * SUB, SUB), SUB)
        h = a_s[rows, :] * h + b_s[rows, :]
        b_s[rows, :] = h
        return h

    h = lax.fori_loop(0, tt, step, state_s[...], unroll=8)
    state_s[...] = h
    return h


def _rg_fwd_kernel(hp_ref, h_ref, hn_ref, sh_ref, sc_ref, ng_ref, win_ref, cw_ref, cb_ref,
                   wa_ref, ba_ref, wi_ref, bi_ref, lam_ref, h0_ref,
                   gate_ref, u_ref, hf_ref, st_ref, a_s, b_s, state_s, *, tt, d, heads):
    t = pl.program_id(1)
    nt = pl.num_programs(1)
    n = tt * SUB

    @pl.when(t == 0)
    def _():
        state_s[...] = h0_ref[...]

    ng = ng_ref[...]
    sc = sc_ref[0]
    sh = sh_ref[0]

    def nm(rows):
        return _norm_mod(rows.reshape(-1, SUB, d), ng, sc, sh).reshape(rows.shape)

    xm = nm(_to_time_major(h_ref[...]))
    xp = jnp.where(t > 0, nm(_to_time_major(hp_ref[...])[(HALO - 1) * SUB:, :]), 0.0)
    xn = jnp.where(t < nt - 1, nm(_to_time_major(hn_ref[...])[:3 * SUB, :]), 0.0)
    xe = jnp.concatenate([xp, xm, xn], axis=0).astype(BF16)

    gate = jnp.dot(xm.astype(BF16), win_ref[:, :d], preferred_element_type=F32)
    gate_ref[0] = gate.astype(gate_ref.dtype)

    ur = jnp.dot(xe, win_ref[:, d:], preferred_element_type=F32)
    cw = cw_ref[...]
    u = cb_ref[...]
    for i in range(cw.shape[0]):
        u = u + cw[i:i + 1, :] * ur[i * SUB:i * SUB + n]
    u_ref[0] = u

    a, b_in = _rg_gates(u, wa_ref, ba_ref, wi_ref, bi_ref, lam_ref, heads)
    a_s[...] = a
    b_s[...] = b_in
    st_ref[...] = _scan_tile(a_s, b_s, state_s, tt, reverse=False)
    hf_ref[0] = b_s[...]


def _rg_bwd_kernel(u_ref, gate_ref, hf_ref, h_ref, g1_ref, wa_ref, ba_ref, wi_ref, bi_ref, lam_ref, h0_ref,
                   wout_ref, o_ref, st_ref, a_s, b_s, state_s, *, tt, d, heads):
    t = pl.program_id(1)

    @pl.when(t == 0)
    def _():
        state_s[...] = h0_ref[...]

    a, b_in = _rg_gates(u_ref[0], wa_ref, ba_ref, wi_ref, bi_ref, lam_ref, heads)
    a_s[...] = a
    b_s[...] = b_in
    st_ref[...] = _scan_tile(a_s, b_s, state_s, tt, reverse=True)

    hsum = hf_ref[0] + b_s[...]
    y = (hsum * _gelu_tanh(gate_ref[0].astype(F32))).astype(BF16)
    out = jnp.dot(y, wout_ref[...], preferred_element_type=F32)
    out = (out.reshape(tt, SUB, d) * g1_ref[0]).reshape(tt * SUB, d)
    o_ref[...] = h_ref[...] + _to_batch_major(out, tt)


def _rg_layer(h, sh, sc, g1, ng, win, cw, cb, wa, ba, wi, bi, lam, wout, h0_f, h0_b):
    b, l, d = h.shape
    heads = wa.shape[1]
    assert b % SUB == 0 and l % HALO == 0, "the recurrent kernels scan SUB batch rows per sublane tile"
    groups = b // SUB
    tt = min(RG_TT, l)
    nt = l // tt
    r = tt // HALO
    n = tt * SUB
    grid = (groups, nt)
    sem = _params(("parallel", "arbitrary"))
    sh, sc, g1 = (v.reshape(groups, SUB, d) for v in (sh, sc, g1))

    vec = pl.BlockSpec((1, SUB, d), lambda gi, ti: (gi, 0, 0))
    st_spec = pl.BlockSpec((SUB, d), lambda gi, ti: (gi, 0))
    tile_f = pl.BlockSpec((SUB, tt, d), lambda gi, ti: (gi, ti, 0))
    tm_f = pl.BlockSpec((1, n, d), lambda gi, ti: (gi, ti, 0))
    prev_spec = pl.BlockSpec((SUB, HALO, d), lambda gi, ti: (gi, jnp.maximum(ti * r - 1, 0), 0))
    next_spec = pl.BlockSpec((SUB, HALO, d), lambda gi, ti: (gi, jnp.minimum((ti + 1) * r, nt * r - 1), 0))
    scratch = [pltpu.VMEM((n, d), F32), pltpu.VMEM((n, d), F32), pltpu.VMEM((SUB, d), F32)]
    tm_shape = (groups, l * SUB, d)

    def dirw(k):
        return wa[k], ba[k], wi[k], bi[k], lam[k]

    fw = dirw(0)
    gate, u, hf, st_f = pl.pallas_call(
        functools.partial(_rg_fwd_kernel, tt=tt, d=d, heads=heads),
        grid=grid,
        in_specs=[prev_spec, tile_f, next_spec, vec, vec, _full(ng), _full(win), _full(cw), _full(cb),
                  *[_full(w) for w in fw], st_spec],
        out_specs=[tm_f, tm_f, tm_f, st_spec],
        out_shape=[jax.ShapeDtypeStruct(tm_shape, BF16), jax.ShapeDtypeStruct(tm_shape, F32),
                   jax.ShapeDtypeStruct(tm_shape, F32), jax.ShapeDtypeStruct((b, d), F32)],
        scratch_shapes=scratch,
        compiler_params=sem,
        name="rg_fwd",
    )(h, h, h, sh, sc, ng, win, cw, cb, *fw, h0_f)

    tile_r = pl.BlockSpec((SUB, tt, d), lambda gi, ti: (gi, nt - 1 - ti, 0))
    tm_r = pl.BlockSpec((1, n, d), lambda gi, ti: (gi, nt - 1 - ti, 0))
    bw = dirw(1)
    h_new, st_b = pl.pallas_call(
        functools.partial(_rg_bwd_kernel, tt=tt, d=d, heads=heads),
        grid=grid,
        in_specs=[tm_r, tm_r, tm_r, tile_r, vec, *[_full(w) for w in bw], st_spec, _full(wout)],
        out_specs=[tile_r, st_spec],
        out_shape=[jax.ShapeDtypeStruct((b, l, d), F32), jax.ShapeDtypeStruct((b, d), F32)],
        scratch_shapes=scratch,
        compiler_params=sem,
        name="rg_bwd",
    )(u, gate, hf, h, g1, *bw, h0_b, wout)
    return h_new, st_f, st_b


def _hy_in_kernel(hp_ref, h_ref, hn_ref, sh_ref, sc_ref, ng_ref, win_ref, cw_ref, cb_ref,
                  x0_ref, xv_ref, *, t_rows, d, ck):
    _, xe = _ext_rows(hp_ref, h_ref, hn_ref, ng_ref[...], sc_ref[0], sh_ref[0])
    offs = (-1, 0, 1)

    def proj(lo):
        z = jnp.dot(xe, win_ref[:, lo:lo + ck], preferred_element_type=F32)
        return _conv_rows(z, cw_ref[:, lo:lo + ck], cb_ref[:, lo:lo + ck], offs, t_rows)

    for ci in range(d // ck):
        lo = ci * ck
        x0_ref[0, :, lo:lo + ck] = proj(lo).astype(BF16)
        xv_ref[0, :, lo:lo + ck] = (proj(d + lo) * proj(2 * d + lo)).astype(BF16)


def _hy_in_call(h, sh, sc, ng, win, cw, cb):
    b, l, d = h.shape
    t_rows = min(ROW_TILE, l)
    ck = _pick(d, COL_CHUNK, LANE)
    vec = pl.BlockSpec((1, 1, d), lambda bi, ti: (bi, 0, 0))
    tile = pl.BlockSpec((1, t_rows, d), lambda bi, ti: (bi, ti, 0))
    return pl.pallas_call(
        functools.partial(_hy_in_kernel, t_rows=t_rows, d=d, ck=ck),
        grid=(b, l // t_rows),
        in_specs=[*_halo_specs(t_rows, l, d), vec, vec, _full(ng), _full(win), _full(cw), _full(cb)],
        out_specs=[tile, tile],
        out_shape=[jax.ShapeDtypeStruct((b, l, d), BF16), jax.ShapeDtypeStruct((b, l, d), BF16)],
        compiler_params=_params(("parallel", "arbitrary")),
    )(h, h, h, sh, sc, ng, win, cw, cb)


def _hy_filter_kernel(w1_ref, b1_ref, w2_ref, b2_ref, w3_ref, b3_ref, w4_ref, fr_ref, k_ref, hdn_s,
                      *, n, d, dc, bands):
    j = pl.program_id(0)
    hp = lax.Precision.HIGHEST

    @pl.when(j == 0)
    def _():
        s = lax.broadcasted_iota(jnp.int32, (n, LANE), 0).astype(F32)
        lane = lax.broadcasted_iota(jnp.int32, (n, LANE), 1)
        tpos = s / float(n - 1)
        w = s * (2.0 * math.pi / n)
        bidx = jnp.where(lane <= bands, lane - 1, lane - 1 - bands).astype(F32)
        band = 1e-4 + bidx * ((bands - 1 - 1e-4) / (bands - 1))
        arg = band * w
        z = jnp.where(lane == 0, tpos,
                      jnp.where(lane <= bands, jnp.cos(arg),
                                jnp.where(lane <= 2 * bands, -jnp.sin(arg), 0.0)))
        fr = fr_ref[...]
        hd = jnp.sin(fr * (jnp.dot(z, w1_ref[...], precision=hp, preferred_element_type=F32) + b1_ref[...]))
        hd = jnp.sin(fr * (jnp.dot(hd, w2_ref[...], precision=hp, preferred_element_type=F32) + b2_ref[...]))
        hd = jnp.sin(fr * (jnp.dot(hd, w3_ref[...], precision=hp, preferred_element_type=F32) + b3_ref[...]))
        hdn_s[...] = hd

    k = jnp.dot(hdn_s[...], w4_ref[...], precision=hp, preferred_element_type=F32)
    srow = lax.broadcasted_iota(jnp.int32, (n, 1), 0).astype(F32)
    centre = n // 2
    dist = jnp.abs(srow - float(centre)) / float(centre)
    ch = (j * dc + lax.broadcasted_iota(jnp.int32, (1, dc), 1)).astype(F32)
    d_lo = math.log(HY_TARGET) / HY_SLOW_DECAY
    d_hi = math.log(HY_TARGET) / HY_FAST_DECAY
    delta = d_lo + ch * ((d_hi - d_lo) / (d - 1))
    k = k * jnp.exp(-dist * jnp.abs(delta))
    k = k / jnp.sum(jnp.abs(k), axis=0, keepdims=True)
    k_ref[...] = k.astype(k_ref.dtype)


def _hy_filter_call(n, w1, b1, w2, b2, w3, b3, w4, fr):
    emb, fd = w1.shape
    d = w4.shape[1]
    bands = (emb - 1) // 2
    w1p = jnp.zeros((LANE, fd), F32).at[:emb].set(w1)
    dc = _pick(d, COL_CHUNK, LANE)
    args = (w1p, b1.reshape(1, fd), w2, b2.reshape(1, fd), w3, b3.reshape(1, fd))
    return pl.pallas_call(
        functools.partial(_hy_filter_kernel, n=n, d=d, dc=dc, bands=bands),
        grid=(d // dc,),
        in_specs=[*[_full(a, single=False) for a in args],
                  pl.BlockSpec((fd, dc), lambda j: (0, j)), _full(fr.reshape(1, fd), single=False)],
        out_specs=pl.BlockSpec((n, dc), lambda j: (0, j)),
        out_shape=jax.ShapeDtypeStruct((n, d), BF16),
        scratch_shapes=[pltpu.VMEM((n, fd), F32)],
        compiler_params=_params(("arbitrary",)),
    )(*args, w4, fr.reshape(1, fd))


def _dft_mats(n):
    i = 2 * jnp.arange(n, dtype=jnp.int32) + 1
    m = (i[:, None] * i[None, :]) % (8 * n)
    ang = m.astype(F32) * (2.0 * math.pi / (8 * n))
    return jnp.cos(ang).astype(BF16), jnp.sin(ang).astype(BF16)


def _hy_spec_kernel(c_ref, s_ref, k_ref, kr_ref, ki_ref, *, n, fb):
    j = pl.program_id(0)
    kk = k_ref[...]
    a = jnp.dot(c_ref[...], kk, preferred_element_type=F32)
    b = jnp.dot(s_ref[...], kk, preferred_element_type=F32)
    fi = 2 * (j * fb + lax.broadcasted_iota(jnp.int32, (fb, 1), 0)) + 1
    m = lax.rem(fi * (n + 1), 8 * n)
    ang = m.astype(F32) * (2.0 * math.pi / (8 * n))
    qr = jnp.cos(ang) * (1.0 / n)
    qi = jnp.sin(ang) * (1.0 / n)
    kr_ref[...] = qr * a + qi * b
    ki_ref[...] = qi * a - qr * b


def _hy_spec_call(cm, sm, k):
    n, d = k.shape
    fb = min(DFT_FB, n)
    row = pl.BlockSpec((fb, n), lambda j: (j, 0))
    out = pl.BlockSpec((fb, d), lambda j: (j, 0))
    return pl.pallas_call(
        functools.partial(_hy_spec_kernel, n=n, fb=fb),
        grid=(n // fb,),
        in_specs=[row, row, _full(k)],
        out_specs=[out, out],
        out_shape=[jax.ShapeDtypeStruct((n, d), F32), jax.ShapeDtypeStruct((n, d), F32)],
        compiler_params=_params(("arbitrary",)),
    )(cm, sm, k)


def _hy_conv_kernel(c_ref, s_ref, xv_ref, kr_ref, ki_ref, x0_ref, h_ref, skip_ref, g1_ref, wout_ref,
                    o_ref, zr_s, zi_s, *, fb):
    p = pl.program_id(1)
    j = pl.program_id(2)
    rows = pl.ds(pl.multiple_of(j * fb, fb), fb)

    @pl.when(p == 0)
    def _():
        xv = xv_ref[0]
        xr = jnp.dot(c_ref[...], xv, preferred_element_type=F32)
        xi = -jnp.dot(s_ref[...], xv, preferred_element_type=F32)
        kr = kr_ref[...]
        ki = ki_ref[...]
        zr_s[rows, :] = (xr * kr - xi * ki).astype(BF16)
        zi_s[rows, :] = (xr * ki + xi * kr).astype(BF16)

    @pl.when(p == 1)
    def _():
        y = (jnp.dot(c_ref[...], zr_s[...], preferred_element_type=F32)
             - jnp.dot(s_ref[...], zi_s[...], preferred_element_type=F32))
        y = y + xv_ref[0, rows, :].astype(F32) * skip_ref[...]
        pv = (x0_ref[0].astype(F32) * y).astype(BF16)
        out = jnp.dot(pv, wout_ref[...], preferred_element_type=F32)
        o_ref[0] = h_ref[0] + g1_ref[0] * out


def _hy_conv_call(cm, sm, xv, kr, ki, x0, h, skip, g1, wout):
    b, n, d = h.shape
    fb = min(DFT_FB, n)
    nf = n // fb
    row = pl.BlockSpec((fb, n), lambda bi, p, j: (j, 0))
    kspec = pl.BlockSpec((fb, d), lambda bi, p, j: (jnp.where(p == 0, j, nf - 1), 0))
    tile = pl.BlockSpec((1, fb, d), lambda bi, p, j: (bi, jnp.where(p == 1, j, 0), 0))
    return pl.pallas_call(
        functools.partial(_hy_conv_kernel, fb=fb),
        grid=(b, 2, nf),
        in_specs=[row, row,
                  pl.BlockSpec((1, n, d), lambda bi, p, j: (bi, 0, 0), pipeline_mode=pl.Buffered(1)),
                  kspec, kspec, tile, tile, _full(skip),
                  pl.BlockSpec((1, 1, d), lambda bi, p, j: (bi, 0, 0)), _full(wout)],
        out_specs=tile,
        out_shape=jax.ShapeDtypeStruct((b, n, d), F32),
        scratch_shapes=[pltpu.VMEM((n, d), BF16), pltpu.VMEM((n, d), BF16)],
        compiler_params=_params(("parallel", "arbitrary", "arbitrary")),
    )(cm, sm, xv, kr, ki, x0, h, skip, g1, wout)


def _to_col_kernel(x_ref, o_ref, *, rows, wb, d):
    for w in range(wb):
        o_ref[0, w * rows:(w + 1) * rows, :] = x_ref[0, :, w * d:(w + 1) * d]


def _to_col_major(x):
    b, l, d = x.shape
    rows = l // GRID_W
    wb = _pick(GRID_W, max(1, ROW_TILE // rows), 1)
    return pl.pallas_call(
        functools.partial(_to_col_kernel, rows=rows, wb=wb, d=d),
        grid=(b, GRID_W // wb),
        in_specs=[pl.BlockSpec((1, rows, wb * d), lambda bi, wi: (bi, 0, wi))],
        out_specs=pl.BlockSpec((1, wb * rows, d), lambda bi, wi: (bi, wi, 0)),
        out_shape=jax.ShapeDtypeStruct((b, l, d), x.dtype),
        compiler_params=_params(("parallel", "arbitrary")),
    )(x.reshape(b, rows, GRID_W * d))


def _final_col_kernel(x_ref, g_ref, o_ref, *, rows, wb, d):
    g = g_ref[...]
    for w in range(wb):
        xw = x_ref[0, w * rows:(w + 1) * rows, :]
        ms = jnp.mean(xw * xw, axis=-1, keepdims=True)
        o_ref[0, :, w * d:(w + 1) * d] = xw * lax.rsqrt(ms + EPS) * g


def _final_row_kernel(x_ref, g_ref, o_ref):
    xw = x_ref[0]
    ms = jnp.mean(xw * xw, axis=-1, keepdims=True)
    o_ref[0] = xw * lax.rsqrt(ms + EPS) * g_ref[...]


def _final_norm(x, g, col_major):
    b, l, d = x.shape
    g2 = g.reshape(1, d)
    if not col_major:
        t_rows = min(ROW_TILE, l)
        tile = pl.BlockSpec((1, t_rows, d), lambda bi, ti: (bi, ti, 0))
        return pl.pallas_call(
            _final_row_kernel, grid=(b, l // t_rows), in_specs=[tile, _full(g2)], out_specs=tile,
            out_shape=jax.ShapeDtypeStruct((b, l, d), F32),
            compiler_params=_params(("parallel", "arbitrary")),
        )(x, g2)
    rows = l // GRID_W
    wb = _pick(GRID_W, max(1, ROW_TILE // rows), 1)
    out = pl.pallas_call(
        functools.partial(_final_col_kernel, rows=rows, wb=wb, d=d),
        grid=(b, GRID_W // wb),
        in_specs=[pl.BlockSpec((1, wb * rows, d), lambda bi, wi: (bi, wi, 0)), _full(g2)],
        out_specs=pl.BlockSpec((1, rows, wb * d), lambda bi, wi: (bi, 0, wi)),
        out_shape=jax.ShapeDtypeStruct((b, rows, GRID_W * d), F32),
        compiler_params=_params(("parallel", "arbitrary")),
    )(x, g2)
    return out.reshape(b, l, d)


def kernel(x, c, ctx, c_ctx, mod_w, mod_b, norm1_g, norm2_g, final_g, rg_w_in, rg_conv_w, rg_conv_b, rg_w_a, rg_b_a, rg_w_i, rg_b_i, rg_lam, rg_w_out, hy_w_in, hy_short_w, hy_short_b, hy_pe_w1, hy_pe_b1, hy_pe_w2, hy_pe_b2, hy_pe_w3, hy_pe_b3, hy_pe_w4, hy_freq, hy_skip, hy_w_out, ffn_w_up, ffn_conv_w, ffn_conv_b, ffn_w_down):
    b, l, d = x.shape
    lc = ctx.shape[1]
    depth = mod_w.shape[0]
    n_mixers = 2
    ctx_needed = [any((q % n_mixers) == 0 for q in range(i + 1, depth)) for i in range(depth)]

    bp = -(-(b + 1) // HALO) * HALO
    cpad = jnp.zeros((bp, d), F32).at[:b].set(c).at[b].set(c_ctx)
    mods = _mod_call(cpad, mod_w, mod_b)

    dft = {}
    h = x
    s = ctx
    col = False
    zeros_state = jnp.zeros((b, d), F32)
    for i in range(depth):
        kind = i % n_mixers
        j = i // n_mixers
        col_major = j % 2 == 1
        keep_ctx = ctx_needed[i]
        if col_major != col:
            assert col_major, "column-major layers are expected to be contiguous at the end"
            h = _to_col_major(h)
            col = True
        lat = [mods[i, :b, q * d:(q + 1) * d].reshape(b, 1, d) for q in range(6)]
        cmod = [jnp.broadcast_to(mods[i, b, q * d:(q + 1) * d].reshape(1, 1, d), (b, 1, d)) for q in range(6)]
        ng1 = norm1_g[i].reshape(1, d)
        ng2 = norm2_g[i].reshape(1, d)

        if kind == 0:
            w = (ng1, rg_w_in[j].astype(BF16), rg_conv_w[j], rg_conv_b[j].reshape(1, d),
                 rg_w_a[j].astype(BF16), rg_b_a[j].reshape(2, 1, d), rg_w_i[j].astype(BF16),
                 rg_b_i[j].reshape(2, 1, d), rg_lam[j].reshape(2, 1, d), rg_w_out[j].astype(BF16))
            s_new, st_f, st_b = _rg_layer(s, cmod[0], cmod[1], cmod[2], *w, zeros_state, zeros_state)
            h, _, _ = _rg_layer(h, lat[0], lat[1], lat[2], *w, st_f, st_b)
            if keep_ctx:
                s = s_new
        else:
            win = hy_w_in[j].astype(BF16)
            cw = hy_short_w[j]
            cb = hy_short_b[j].reshape(1, 3 * d)
            wout = hy_w_out[j].astype(BF16)
            skip = hy_skip[j].reshape(1, d)
            pe = (hy_pe_w1[j], hy_pe_b1[j], hy_pe_w2[j], hy_pe_b2[j], hy_pe_w3[j], hy_pe_b3[j], hy_pe_w4[j],
                  hy_freq[j])
            streams = [(h, lat, True)] + ([(s, cmod, False)] if keep_ctx else [])
            for stream, md, is_lat in streams:
                n = stream.shape[1]
                if n not in dft:
                    dft[n] = _dft_mats(n)
                cm, sm = dft[n]
                kr, ki = _hy_spec_call(cm, sm, _hy_filter_call(n, *pe))
                x0, xv = _hy_in_call(stream, md[0], md[1], ng1, win, cw, cb)
                new = _hy_conv_call(cm, sm, xv, kr, ki, x0, stream, skip, md[2], wout)
                if is_lat:
                    h = new
                else:
                    s = new

        ffn = (ng2, ffn_w_up[i].astype(BF16), ffn_conv_w[i], ffn_conv_b[i].reshape(1, -1), ffn_w_down[i].astype(BF16))
        h = _ffn_call(h, lat[3], lat[4], lat[5], *ffn)
        if keep_ctx:
            s = _ffn_call(s, cmod[3], cmod[4], cmod[5], *ffn)
    return _final_norm(h, final_g, col)
```

```python
import functools
import math

import jax
import jax.numpy as jnp
from jax import lax
from jax.experimental import pallas as pl
from jax.experimental.pallas import tpu as pltpu

F32 = jnp.float32
BF16 = jnp.bfloat16

GRID_W = 64
RG_C = 8.0
HY_FAST_DECAY = 0.3
HY_SLOW_DECAY = 1.5
HY_TARGET = 1e-2
EPS = 1e-6
TINY = 1e-30

SUB = 8
LANE = 128
HALO = SUB
ROW_TILE = 512
RG_TT = 64
DFT_FB = 256
COL_CHUNK = 256
FFT_P = 256
FFT_R = 16
VMEM_LIMIT = 56 * 1024 * 1024


def _pick(n, cap, mult):
    best = None
    for d in range(mult, min(n, cap) + 1, mult):
        if n % d == 0:
            best = d
    return n if best is None else best


def _full(arr, single=True):
    nd = arr.ndim
    if single:
        return pl.BlockSpec(arr.shape, lambda *_: (0,) * nd, pipeline_mode=pl.Buffered(1))
    return pl.BlockSpec(arr.shape, lambda *_: (0,) * nd)


def _params(sem):
    return pltpu.CompilerParams(dimension_semantics=sem, vmem_limit_bytes=VMEM_LIMIT)


def _sigmoid(x):
    return 0.5 + 0.5 * jnp.tanh(0.5 * x)


def _gelu_tanh(x):
    return 0.5 * x * (1.0 + jnp.tanh(math.sqrt(2.0 / math.pi) * (x + 0.044715 * (x * x * x))))


def _norm_mod(x, ng, sc, sh):
    ms = jnp.mean(x * x, axis=-1, keepdims=True)
    y = x * lax.rsqrt(ms + EPS)
    return (y * ng) * (1.0 + sc) + sh


def _shift_rows(z, k):
    if k == 0:
        return z
    return pltpu.roll(z, (-k) % z.shape[0], 0)


def _conv_rows(z, cw, cb, offsets, t):
    acc = None
    for i, o in enumerate(offsets):
        term = cw[i:i + 1, :] * _shift_rows(z, o)[HALO:HALO + t]
        acc = term if acc is None else acc + term
    return acc + cb


def _halo_specs(t_rows, l, d):
    r = t_rows // HALO
    last = l // HALO - 1

    def prev_map(b, t):
        return (b, jnp.maximum(t * r - 1, 0), 0)

    def next_map(b, t):
        return (b, jnp.minimum((t + 1) * r, last), 0)

    return (pl.BlockSpec((1, HALO, d), prev_map),
            pl.BlockSpec((1, t_rows, d), lambda b, t: (b, t, 0)),
            pl.BlockSpec((1, HALO, d), next_map))


def _ext_rows(hp_ref, h_ref, hn_ref, ng, sc, sh):
    t = pl.program_id(1)
    nt = pl.num_programs(1)
    x = h_ref[0]
    xm = _norm_mod(x, ng, sc, sh)
    xp = jnp.where(t > 0, _norm_mod(hp_ref[0], ng, sc, sh), 0.0)
    xn = jnp.where(t < nt - 1, _norm_mod(hn_ref[0], ng, sc, sh), 0.0)
    return x, jnp.concatenate([xp, xm, xn], axis=0).astype(BF16)


def _mod_kernel(c_ref, w_ref, b_ref, o_ref):
    cv = c_ref[...]
    ca = cv * _sigmoid(cv)
    o_ref[0] = jnp.dot(ca.astype(BF16), w_ref[0].astype(BF16), preferred_element_type=F32) + b_ref[0]


def _mod_call(cpad, mod_w, mod_b):
    depth, d, m6 = mod_w.shape
    bp = cpad.shape[0]
    tn = _pick(m6, 1536, LANE)
    return pl.pallas_call(
        _mod_kernel,
        grid=(depth, m6 // tn),
        in_specs=[pl.BlockSpec((bp, d), lambda i, j: (0, 0)),
                  pl.BlockSpec((1, d, tn), lambda i, j: (i, 0, j)),
                  pl.BlockSpec((1, 1, tn), lambda i, j: (i, 0, j))],
        out_specs=pl.BlockSpec((1, bp, tn), lambda i, j: (i, 0, j)),
        out_shape=jax.ShapeDtypeStruct((depth, bp, m6), F32),
        compiler_params=_params(("arbitrary", "arbitrary")),
        name="modulation",
    )(cpad, mod_w, mod_b.reshape(depth, 1, m6))


def _ffn_kernel(hp_ref, h_ref, hn_ref, sh_ref, sc_ref, g_ref, ng_ref, wup_ref, cw_ref, cb_ref, wdn_ref,
                o_ref, act_s, *, t_rows, f, ck):
    x, xe = _ext_rows(hp_ref, h_ref, hn_ref, ng_ref[...], sc_ref[0], sh_ref[0])
    offs = (-1, 0, 1)
    for ci in range(f // ck):
        lo = ci * ck
        zg = jnp.dot(xe, wup_ref[:, lo:lo + ck], preferred_element_type=F32)
        zu = jnp.dot(xe, wup_ref[:, f + lo:f + lo + ck], preferred_element_type=F32)
        g = _conv_rows(zg, cw_ref[:, lo:lo + ck], cb_ref[:, lo:lo + ck], offs, t_rows)
        u = _conv_rows(zu, cw_ref[:, f + lo:f + lo + ck], cb_ref[:, f + lo:f + lo + ck], offs, t_rows)
        act_s[:, lo:lo + ck] = (g * _sigmoid(g) * u).astype(BF16)
    out = jnp.dot(act_s[...], wdn_ref[...], preferred_element_type=F32)
    o_ref[0] = x + g_ref[0] * out


def _ffn_call(h, sh, sc, g, ng, wup, cw, cb, wdn):
    b, l, d = h.shape
    f = wdn.shape[0]
    t_rows = min(ROW_TILE, l)
    ck = _pick(f, COL_CHUNK, LANE)
    vec = pl.BlockSpec((1, 1, d), lambda bi, ti: (bi, 0, 0))
    kern = functools.partial(_ffn_kernel, t_rows=t_rows, f=f, ck=ck)
    return pl.pallas_call(
        kern,
        grid=(b, l // t_rows),
        in_specs=[*_halo_specs(t_rows, l, d), vec, vec, vec, _full(ng), _full(wup), _full(cw), _full(cb), _full(wdn)],
        out_specs=pl.BlockSpec((1, t_rows, d), lambda bi, ti: (bi, ti, 0)),
        out_shape=jax.ShapeDtypeStruct((b, l, d), F32),
        scratch_shapes=[pltpu.VMEM((t_rows, f), BF16)],
        compiler_params=_params(("parallel", "arbitrary")),
        name="conv_ffn",
    )(h, h, h, sh, sc, g, ng, wup, cw, cb, wdn)


def _rg_gates(u2, wa_ref, ba_ref, wi_ref, bi_ref, lam_ref, heads):
    d = u2.shape[1]
    hd = d // heads
    ub = u2.astype(BF16)

    def block_diag(w_ref):
        return jnp.concatenate(
            [jnp.dot(ub[:, k * hd:(k + 1) * hd], w_ref[k], preferred_element_type=F32) for k in range(heads)],
            axis=1)

    r = _sigmoid(block_diag(wa_ref) + ba_ref[...])
    gi = _sigmoid(block_diag(wi_ref) + bi_ref[...])
    lam = lam_ref[...]
    softplus_neg = jnp.maximum(-lam, 0.0) + jnp.log1p(jnp.exp(-jnp.abs(lam)))
    log_a = (-RG_C) * r * softplus_neg
    a = jnp.exp(log_a)
    om = -jnp.tanh(log_a) * (1.0 + a * a)
    beta = om * lax.rsqrt(jnp.maximum(om, TINY))
    return a, beta * gi * u2


def _swap_slab_sublane(slabs):
    sub = lax.broadcasted_iota(jnp.int32, slabs[0].shape, 1)
    s = SUB // 2
    while s:
        bit = (sub & s) != 0
        new = list(slabs)
        for k in range(SUB):
            if k & s:
                continue
            lo, hi = slabs[k], slabs[k | s]
            new[k] = jnp.where(bit, pltpu.roll(hi, s, 1), lo)
            new[k | s] = jnp.where(bit, hi, pltpu.roll(lo, SUB - s, 1))
        slabs = new
        s //= 2
    return slabs


def _to_time_major(x):
    _, t, c = x.shape
    slabs = _swap_slab_sublane([x[k].reshape(t // SUB, SUB, c) for k in range(SUB)])
    return jnp.stack(slabs, axis=1).reshape(t * SUB, c)


def _to_batch_major(y, t):
    c = y.shape[1]
    y4 = y.reshape(t // SUB, SUB, SUB, c)
    slabs = _swap_slab_sublane([y4[:, k] for k in range(SUB)])
    return jnp.stack([s.reshape(t, c) for s in slabs], axis=0)


def _scan_tile(a_s, b_s, state_s, tt, reverse):
    def step(s, h):
        pos = (tt - 1 - s) if reverse else s
        rows = pl.ds(pl.multiple_of(pos * SUB, SUB), SUB)
        h = a_s[rows, :] * h + b_s[rows, :]
        b_s[rows, :] = h
        return h

    h = lax.fori_loop(0, tt, step, state_s[...], unroll=8)
    state_s[...] = h
    return h


def _rg_fwd_kernel(hp_ref, h_ref, hn_ref, sh_ref, sc_ref, ng_ref, win_ref, cw_ref, cb_ref,
                   wa_ref, ba_ref, wi_ref, bi_ref, lam_ref, h0_ref,
                   gate_ref, u_ref, hf_ref, st_ref, a_s, b_s, state_s, *, tt, d, heads):
    t = pl.program_id(1)
    nt = pl.num_programs(1)
    n = tt * SUB

    @pl.when(t == 0)
    def _():
        state_s[...] = h0_ref[...]

    ng = ng_ref[...]
    sc = sc_ref[0]
    sh = sh_ref[0]

    def nm(rows):
        return _norm_mod(rows.reshape(-1, SUB, d), ng, sc, sh).reshape(rows.shape)

    xm = nm(_to_time_major(h_ref[...]))
    xp = jnp.where(t > 0, nm(_to_time_major(hp_ref[...])[(HALO - 1) * SUB:, :]), 0.0)
    xn = jnp.where(t < nt - 1, nm(_to_time_major(hn_ref[...])[:3 * SUB, :]), 0.0)
    xe = jnp.concatenate([xp, xm, xn], axis=0).astype(BF16)

    gate = jnp.dot(xm.astype(BF16), win_ref[:, :d], preferred_element_type=F32)
    gate_ref[0] = gate.astype(gate_ref.dtype)

    ur = jnp.dot(xe, win_ref[:, d:], preferred_element_type=F32)
    cw = cw_ref[...]
    u = cb_ref[...]
    for i in range(cw.shape[0]):
        u = u + cw[i:i + 1, :] * ur[i * SUB:i * SUB + n]
    u_ref[0] = u

    a, b_in = _rg_gates(u, wa_ref, ba_ref, wi_ref, bi_ref, lam_ref, heads)
    a_s[...] = a
    b_s[...] = b_in
    st_ref[...] = _scan_tile(a_s, b_s, state_s, tt, reverse=False)
    hf_ref[0] = b_s[...]


def _rg_bwd_kernel(u_ref, gate_ref, hf_ref, h_ref, g1_ref, wa_ref, ba_ref, wi_ref, bi_ref, lam_ref, h0_ref,
                   wout_ref, o_ref, st_ref, a_s, b_s, state_s, *, tt, d, heads):
    t = pl.program_id(1)

    @pl.when(t == 0)
    def _():
        state_s[...] = h0_ref[...]

    a, b_in = _rg_gates(u_ref[0], wa_ref, ba_ref, wi_ref, bi_ref, lam_ref, heads)
    a_s[...] = a
    b_s[...] = b_in
    st_ref[...] = _scan_tile(a_s, b_s, state_s, tt, reverse=True)

    hsum = hf_ref[0] + b_s[...]
    y = (hsum * _gelu_tanh(gate_ref[0].astype(F32))).astype(BF16)
    out = jnp.dot(y, wout_ref[...], preferred_element_type=F32)
    out = (out.reshape(tt, SUB, d) * g1_ref[0]).reshape(tt * SUB, d)
    o_ref[...] = h_ref[...] + _to_batch_major(out, tt)


def _rg_layer(h, sh, sc, g1, ng, win, cw, cb, wa, ba, wi, bi, lam, wout, h0_f, h0_b):
    b, l, d = h.shape
    heads = wa.shape[1]
    assert b % SUB == 0 and l % HALO == 0, "the recurrent kernels scan SUB batch rows per sublane tile"
    groups = b // SUB
    tt = min(RG_TT, l)
    nt = l // tt
    r = tt // HALO
    n = tt * SUB
    grid = (groups, nt)
    sem = _params(("parallel", "arbitrary"))
    sh, sc, g1 = (v.reshape(groups, SUB, d) for v in (sh, sc, g1))

    vec = pl.BlockSpec((1, SUB, d), lambda gi, ti: (gi, 0, 0))
    st_spec = pl.BlockSpec((SUB, d), lambda gi, ti: (gi, 0))
    tile_f = pl.BlockSpec((SUB, tt, d), lambda gi, ti: (gi, ti, 0))
    tm_f = pl.BlockSpec((1, n, d), lambda gi, ti: (gi, ti, 0))
    prev_spec = pl.BlockSpec((SUB, HALO, d), lambda gi, ti: (gi, jnp.maximum(ti * r - 1, 0), 0))
    next_spec = pl.BlockSpec((SUB, HALO, d), lambda gi, ti: (gi, jnp.minimum((ti + 1) * r, nt * r - 1), 0))
    scratch = [pltpu.VMEM((n, d), F32), pltpu.VMEM((n, d), F32), pltpu.VMEM((SUB, d), F32)]
    tm_shape = (groups, l * SUB, d)

    def dirw(k):
        return wa[k], ba[k], wi[k], bi[k], lam[k]

    fw = dirw(0)
    gate, u, hf, st_f = pl.pallas_call(
        functools.partial(_rg_fwd_kernel, tt=tt, d=d, heads=heads),
        grid=grid,
        in_specs=[prev_spec, tile_f, next_spec, vec, vec, _full(ng), _full(win), _full(cw), _full(cb),
                  *[_full(w) for w in fw], st_spec],
        out_specs=[tm_f, tm_f, tm_f, st_spec],
        out_shape=[jax.ShapeDtypeStruct(tm_shape, BF16), jax.ShapeDtypeStruct(tm_shape, F32),
                   jax.ShapeDtypeStruct(tm_shape, F32), jax.ShapeDtypeStruct((b, d), F32)],
        scratch_shapes=scratch,
        compiler_params=sem,
        name="rg_fwd",
    )(h, h, h, sh, sc, ng, win, cw, cb, *fw, h0_f)

    tile_r = pl.BlockSpec((SUB, tt, d), lambda gi, ti: (gi, nt - 1 - ti, 0))
    tm_r = pl.BlockSpec((1, n, d), lambda gi, ti: (gi, nt - 1 - ti, 0))
    bw = dirw(1)
    h_new, st_b = pl.pallas_call(
        functools.partial(_rg_bwd_kernel, tt=tt, d=d, heads=heads),
        grid=grid,
        in_specs=[tm_r, tm_r, tm_r, tile_r, vec, *[_full(w) for w in bw], st_spec, _full(wout)],
        out_specs=[tile_r, st_spec],
        out_shape=[jax.ShapeDtypeStruct((b, l, d), F32), jax.ShapeDtypeStruct((b, d), F32)],
        scratch_shapes=scratch,
        compiler_params=sem,
        name="rg_bwd",
    )(u, gate, hf, h, g1, *bw, h0_b, wout)
    return h_new, st_f, st_b


def _hy_in_kernel(hp_ref, h_ref, hn_ref, sh_ref, sc_ref, ng_ref, win_ref, cw_ref, cb_ref,
                  x0_ref, xv_ref, *, t_rows, d, ck):
    _, xe = _ext_rows(hp_ref, h_ref, hn_ref, ng_ref[...], sc_ref[0], sh_ref[0])
    offs = (-1, 0, 1)

    def proj(lo):
        z = jnp.dot(xe, win_ref[:, lo:lo + ck], preferred_element_type=F32)
        return _conv_rows(z, cw_ref[:, lo:lo + ck], cb_ref[:, lo:lo + ck], offs, t_rows)

    for ci in range(d // ck):
        lo = ci * ck
        x0_ref[0, :, lo:lo + ck] = proj(lo).astype(BF16)
        xv_ref[0, :, lo:lo + ck] = (proj(d + lo) * proj(2 * d + lo)).astype(BF16)


def _hy_in_call(h, sh, sc, ng, win, cw, cb):
    b, l, d = h.shape
    t_rows = min(ROW_TILE, l)
    ck = _pick(d, COL_CHUNK, LANE)
    vec = pl.BlockSpec((1, 1, d), lambda bi, ti: (bi, 0, 0))
    tile = pl.BlockSpec((1, t_rows, d), lambda bi, ti: (bi, ti, 0))
    return pl.pallas_call(
        functools.partial(_hy_in_kernel, t_rows=t_rows, d=d, ck=ck),
        grid=(b, l // t_rows),
        in_specs=[*_halo_specs(t_rows, l, d), vec, vec, _full(ng), _full(win), _full(cw), _full(cb)],
        out_specs=[tile, tile],
        out_shape=[jax.ShapeDtypeStruct((b, l, d), BF16), jax.ShapeDtypeStruct((b, l, d), BF16)],
        compiler_params=_params(("parallel", "arbitrary")),
        name="hy_in",
    )(h, h, h, sh, sc, ng, win, cw, cb)


def _hy_filter_kernel(w1_ref, b1_ref, w2_ref, b2_ref, w3_ref, b3_ref, w4_ref, fr_ref, k_ref, hdn_s,
                      *, n, d, dc, bands):
    j = pl.program_id(0)
    hp = lax.Precision.HIGHEST

    @pl.when(j == 0)
    def _():
        s = lax.broadcasted_iota(jnp.int32, (n, LANE), 0).astype(F32)
        lane = lax.broadcasted_iota(jnp.int32, (n, LANE), 1)
        tpos = s / float(n - 1)
        w = s * (2.0 * math.pi / n)
        bidx = jnp.where(lane <= bands, lane - 1, lane - 1 - bands).astype(F32)
        band = 1e-4 + bidx * ((bands - 1 - 1e-4) / (bands - 1))
        arg = band * w
        z = jnp.where(lane == 0, tpos,
                      jnp.where(lane <= bands, jnp.cos(arg),
                                jnp.where(lane <= 2 * bands, -jnp.sin(arg), 0.0)))
        fr = fr_ref[...]
        hd = jnp.sin(fr * (jnp.dot(z, w1_ref[...], precision=hp, preferred_element_type=F32) + b1_ref[...]))
        hd = jnp.sin(fr * (jnp.dot(hd, w2_ref[...], precision=hp, preferred_element_type=F32) + b2_ref[...]))
        hd = jnp.sin(fr * (jnp.dot(hd, w3_ref[...], precision=hp, preferred_element_type=F32) + b3_ref[...]))
        hdn_s[...] = hd

    k = jnp.dot(hdn_s[...], w4_ref[...], precision=hp, preferred_element_type=F32)
    srow = lax.broadcasted_iota(jnp.int32, (n, 1), 0).astype(F32)
    centre = n // 2
    dist = jnp.abs(srow - float(centre)) / float(centre)
    ch = (j * dc + lax.broadcasted_iota(jnp.int32, (1, dc), 1)).astype(F32)
    d_lo = math.log(HY_TARGET) / HY_SLOW_DECAY
    d_hi = math.log(HY_TARGET) / HY_FAST_DECAY
    delta = d_lo + ch * ((d_hi - d_lo) / (d - 1))
    k = k * jnp.exp(-dist * jnp.abs(delta))
    k = k / jnp.sum(jnp.abs(k), axis=0, keepdims=True)
    k_ref[...] = k.astype(k_ref.dtype)


def _hy_filter_call(n, w1, b1, w2, b2, w3, b3, w4, fr):
    emb, fd = w1.shape
    d = w4.shape[1]
    bands = (emb - 1) // 2
    w1p = jnp.zeros((LANE, fd), F32).at[:emb].set(w1)
    dc = _pick(d, COL_CHUNK, LANE)
    args = (w1p, b1.reshape(1, fd), w2, b2.reshape(1, fd), w3, b3.reshape(1, fd))
    return pl.pallas_call(
        functools.partial(_hy_filter_kernel, n=n, d=d, dc=dc, bands=bands),
        grid=(d // dc,),
        in_specs=[*[_full(a, single=False) for a in args],
                  pl.BlockSpec((fd, dc), lambda j: (0, j)), _full(fr.reshape(1, fd), single=False)],
        out_specs=pl.BlockSpec((n, dc), lambda j: (0, j)),
        out_shape=jax.ShapeDtypeStruct((n, d), BF16),
        scratch_shapes=[pltpu.VMEM((n, fd), F32)],
        compiler_params=_params(("arbitrary",)),
        name="hy_filter",
    )(*args, w4, fr.reshape(1, fd))


def _dft_mats(n):
    i = 2 * jnp.arange(n, dtype=jnp.int32) + 1
    m = (i[:, None] * i[None, :]) % (8 * n)
    ang = m.astype(F32) * (2.0 * math.pi / (8 * n))
    return jnp.cos(ang).astype(BF16), jnp.sin(ang).astype(BF16)


def _hy_spec_kernel(c_ref, s_ref, k_ref, kr_ref, ki_ref, *, n, fb):
    j = pl.program_id(0)
    kk = k_ref[...]
    a = jnp.dot(c_ref[...], kk, preferred_element_type=F32)
    b = jnp.dot(s_ref[...], kk, preferred_element_type=F32)
    fi = 2 * (j * fb + lax.broadcasted_iota(jnp.int32, (fb, 1), 0)) + 1
    m = lax.rem(fi * (n + 1), 8 * n)
    ang = m.astype(F32) * (2.0 * math.pi / (8 * n))
    qr = jnp.cos(ang) * (1.0 / n)
    qi = jnp.sin(ang) * (1.0 / n)
    kr_ref[...] = qr * a + qi * b
    ki_ref[...] = qi * a - qr * b


def _hy_spec_call(cm, sm, k):
    n, d = k.shape
    fb = min(DFT_FB, n)
    row = pl.BlockSpec((fb, n), lambda j: (j, 0))
    out = pl.BlockSpec((fb, d), lambda j: (j, 0))
    return pl.pallas_call(
        functools.partial(_hy_spec_kernel, n=n, fb=fb),
        grid=(n // fb,),
        in_specs=[row, row, _full(k)],
        out_specs=[out, out],
        out_shape=[jax.ShapeDtypeStruct((n, d), F32), jax.ShapeDtypeStruct((n, d), F32)],
        compiler_params=_params(("arbitrary",)),
        name="hy_spec_dense",
    )(cm, sm, k)


def _hy_conv_kernel(c_ref, s_ref, xv_ref, kr_ref, ki_ref, x0_ref, h_ref, skip_ref, g1_ref, wout_ref,
                    o_ref, zr_s, zi_s, *, fb):
    p = pl.program_id(1)
    j = pl.program_id(2)
    rows = pl.ds(pl.multiple_of(j * fb, fb), fb)

    @pl.when(p == 0)
    def _():
        xv = xv_ref[0]
        xr = jnp.dot(c_ref[...], xv, preferred_element_type=F32)
        xi = -jnp.dot(s_ref[...], xv, preferred_element_type=F32)
        kr = kr_ref[...]
        ki = ki_ref[...]
        zr_s[rows, :] = (xr * kr - xi * ki).astype(BF16)
        zi_s[rows, :] = (xr * ki + xi * kr).astype(BF16)

    @pl.when(p == 1)
    def _():
        y = (jnp.dot(c_ref[...], zr_s[...], preferred_element_type=F32)
             - jnp.dot(s_ref[...], zi_s[...], preferred_element_type=F32))
        y = y + xv_ref[0, rows, :].astype(F32) * skip_ref[...]
        pv = (x0_ref[0].astype(F32) * y).astype(BF16)
        out = jnp.dot(pv, wout_ref[...], preferred_element_type=F32)
        o_ref[0] = h_ref[0] + g1_ref[0] * out


def _hy_conv_call(cm, sm, xv, kr, ki, x0, h, skip, g1, wout):
    b, n, d = h.shape
    fb = min(DFT_FB, n)
    nf = n // fb
    row = pl.BlockSpec((fb, n), lambda bi, p, j: (j, 0))
    kspec = pl.BlockSpec((fb, d), lambda bi, p, j: (jnp.where(p == 0, j, nf - 1), 0))
    tile = pl.BlockSpec((1, fb, d), lambda bi, p, j: (bi, jnp.where(p == 1, j, 0), 0))
    return pl.pallas_call(
        functools.partial(_hy_conv_kernel, fb=fb),
        grid=(b, 2, nf),
        in_specs=[row, row,
                  pl.BlockSpec((1, n, d), lambda bi, p, j: (bi, 0, 0), pipeline_mode=pl.Buffered(1)),
                  kspec, kspec, tile, tile, _full(skip),
                  pl.BlockSpec((1, 1, d), lambda bi, p, j: (bi, 0, 0)), _full(wout)],
        out_specs=tile,
        out_shape=jax.ShapeDtypeStruct((b, n, d), F32),
        scratch_shapes=[pltpu.VMEM((n, d), BF16), pltpu.VMEM((n, d), BF16)],
        compiler_params=_params(("parallel", "arbitrary", "arbitrary")),
        name="hy_conv_dense",
    )(cm, sm, xv, kr, ki, x0, h, skip, g1, wout)


def _fft_mats(n):
    p, r = FFT_P, FFT_R
    nn = 2 * n
    q, qh, hh = nn // p, n // p, p // r
    i32 = jnp.int32
    t1h = jnp.arange(hh, dtype=i32).reshape(hh, 1, 1, 1)
    f2 = jnp.arange(q, dtype=i32).reshape(1, q, 1, 1)
    t1l = jnp.arange(r, dtype=i32).reshape(1, 1, r, 1)
    t2 = jnp.arange(qh, dtype=i32).reshape(1, 1, 1, qh)
    m = ((2 * f2 + 1) * (p * t2 + r * t1h + t1l)) % (2 * nn)
    th = m.astype(F32) * (math.pi / nn)
    base = jnp.stack([jnp.cos(th), -jnp.sin(th)], axis=1)
    eye = jnp.eye(r, dtype=F32)
    w1 = base[..., None] * eye[None, None, None, :, None, :]
    w1f = w1.reshape(hh, 2 * q * r, qh * r).astype(BF16)
    w1i = jnp.transpose(w1, (0, 4, 5, 1, 2, 3)).reshape(hh, qh * r, 2 * q * r).astype(BF16)
    f1 = jnp.arange(p // 2, dtype=i32)[:, None]
    t1 = jnp.arange(p, dtype=i32)[None, :]
    phi = ((f1 * t1) % p).astype(F32) * (2.0 * math.pi / p)
    c, sn = jnp.cos(phi), jnp.sin(phi)
    w2 = jnp.block([[c, sn], [-sn, c]]).astype(BF16)
    w2i = jnp.block([[c.T, -sn.T], [sn.T, c.T]]).astype(BF16)
    return w1f, w1i, w2, w2i


def _fft_coarse_fwd(load_rows, w1_ref, s_ref, n):
    p, r = FFT_P, FFT_R
    q, qh, hh = 2 * n // p, n // p, p // r

    def body(t1h, carry):
        u = jnp.concatenate([load_rows(pl.multiple_of(p * t2 + t1h * r, r)) for t2 in range(qh)], axis=0)
        res = jnp.dot(w1_ref[t1h], u, preferred_element_type=F32)
        for pf in range(2 * q):
            s_ref[pf * hh + t1h] = res[pf * r:(pf + 1) * r].astype(BF16)
        return carry

    lax.fori_loop(0, hh, body, 0)


def _fft_fine_fwd(s_ref, w2_ref, f2, n):
    p, r = FFT_P, FFT_R
    q, hh = 2 * n // p, p // r
    cc = s_ref.shape[-1]
    v = jnp.concatenate([s_ref[pl.ds(f2 * hh, hh)].reshape(p, cc),
                         s_ref[pl.ds((q + f2) * hh, hh)].reshape(p, cc)], axis=0)
    return jnp.dot(w2_ref[...], v, preferred_element_type=F32)


def _hy_kspec2_kernel(w1_ref, w2_ref, k_ref, o_ref, s_ref, *, n):
    p = FFT_P
    q = 2 * n // p
    _fft_coarse_fwd(lambda r0: k_ref[pl.ds(r0, FFT_R), :], w1_ref, s_ref, n)

    def body(f2, carry):
        xs = _fft_fine_fwd(s_ref, w2_ref, f2, n)
        xr, xi = xs[:p // 2], xs[p // 2:]
        f1 = lax.broadcasted_iota(jnp.int32, (p // 2, 1), 0)
        m = lax.rem(2 * (q * f1 + f2) + 1, 8)
        ang = m.astype(F32) * (math.pi / 4.0)
        qr = jnp.cos(ang) * (1.0 / n)
        qi = jnp.sin(ang) * (1.0 / n)
        o_ref[f2, 0] = qr * xr - qi * xi
        o_ref[f2, 1] = qr * xi + qi * xr
        return carry

    lax.fori_loop(0, q, body, 0)


def _hy_kspec2_call(w1, w2, k):
    n, d = k.shape
    p, r = FFT_P, FFT_R
    q, hh = 2 * n // p, p // r
    cc = _pick(d, COL_CHUNK, LANE)
    return pl.pallas_call(
        functools.partial(_hy_kspec2_kernel, n=n),
        grid=(d // cc,),
        in_specs=[_full(w1), _full(w2), pl.BlockSpec((n, cc), lambda c: (0, c))],
        out_specs=pl.BlockSpec((q, 2, p // 2, cc), lambda c: (0, 0, 0, c)),
        out_shape=jax.ShapeDtypeStruct((q, 2, p // 2, d), F32),
        scratch_shapes=[pltpu.VMEM((2 * q * hh, r, cc), BF16)],
        compiler_params=_params(("arbitrary",)),
        name="hy_kspec",
    )(w1, w2, k)


def _hy_conv2_kernel(w1_ref, w1i_ref, w2_ref, w2i_ref, xv_ref, x0_ref, k_ref, skip_ref, p_ref, s_ref, *, n):
    p, r = FFT_P, FFT_R
    q, qh, hh = 2 * n // p, n // p, p // r
    _fft_coarse_fwd(lambda r0: xv_ref[0, pl.ds(r0, r), :], w1_ref, s_ref, n)

    def mid(f2, carry):
        xs = _fft_fine_fwd(s_ref, w2_ref, f2, n)
        xr, xi = xs[:p // 2], xs[p // 2:]
        kr = k_ref[f2, 0]
        ki = k_ref[f2, 1]
        z = jnp.concatenate([xr * kr - xi * ki, xr * ki + xi * kr], axis=0).astype(BF16)
        c = jnp.dot(w2i_ref[...], z, preferred_element_type=F32)
        cc = c.shape[-1]
        s_ref[pl.ds(f2 * hh, hh)] = c[:p].reshape(hh, r, cc).astype(BF16)
        s_ref[pl.ds((q + f2) * hh, hh)] = c[p:].reshape(hh, r, cc).astype(BF16)
        return carry

    lax.fori_loop(0, q, mid, 0)
    skip = skip_ref[...]

    def back(t1h, carry):
        g = jnp.concatenate([s_ref[pf * hh + t1h] for pf in range(2 * q)], axis=0)
        y = jnp.dot(w1i_ref[t1h], g, preferred_element_type=F32)
        for t2 in range(qh):
            rows = pl.ds(pl.multiple_of(p * t2 + t1h * r, r), r)
            yv = y[t2 * r:(t2 + 1) * r] + xv_ref[0, rows, :].astype(F32) * skip
            p_ref[0, rows, :] = (x0_ref[0, rows, :].astype(F32) * yv).astype(p_ref.dtype)
        return carry

    lax.fori_loop(0, hh, back, 0)


def _hy_conv2_call(mats, xv, x0, kspec, skip):
    b, n, d = xv.shape
    p, r = FFT_P, FFT_R
    q, hh = 2 * n // p, p // r
    cc = _pick(d, COL_CHUNK, LANE)
    w1, w1i, w2, w2i = mats
    col = pl.BlockSpec((1, n, cc), lambda c, bi: (bi, 0, c))
    return pl.pallas_call(
        functools.partial(_hy_conv2_kernel, n=n),
        grid=(d // cc, b),
        in_specs=[_full(w1), _full(w1i), _full(w2), _full(w2i), col, col,
                  pl.BlockSpec((q, 2, p // 2, cc), lambda c, bi: (0, 0, 0, c), pipeline_mode=pl.Buffered(1)),
                  pl.BlockSpec((1, cc), lambda c, bi: (0, c))],
        out_specs=col,
        out_shape=jax.ShapeDtypeStruct((b, n, d), BF16),
        scratch_shapes=[pltpu.VMEM((2 * q * hh, r, cc), BF16)],
        compiler_params=_params(("arbitrary", "arbitrary")),
        name="hy_conv2",
    )(w1, w1i, w2, w2i, xv, x0, kspec, skip)


def _hy_out_kernel(p_ref, h_ref, g1_ref, wout_ref, o_ref):
    o_ref[0] = h_ref[0] + g1_ref[0] * jnp.dot(p_ref[0], wout_ref[...], preferred_element_type=F32)


def _hy_out_call(pv, h, g1, wout):
    b, l, d = h.shape
    t_rows = min(2 * ROW_TILE, l)
    tile = pl.BlockSpec((1, t_rows, d), lambda bi, ti: (bi, ti, 0))
    return pl.pallas_call(
        _hy_out_kernel,
        grid=(b, l // t_rows),
        in_specs=[tile, tile, pl.BlockSpec((1, 1, d), lambda bi, ti: (bi, 0, 0)), _full(wout)],
        out_specs=tile,
        out_shape=jax.ShapeDtypeStruct((b, l, d), F32),
        compiler_params=_params(("parallel", "arbitrary")),
        name="hy_out",
    )(pv, h, g1, wout)


def _to_col_kernel(x_ref, o_ref, *, rows, d):
    x4 = x_ref[0].reshape(rows // SUB, SUB, SUB, d)
    res = _swap_slab_sublane([x4[:, k] for k in range(SUB)])
    for w in range(SUB):
        o_ref[0, w * rows:(w + 1) * rows, :] = res[w].reshape(rows, d)


def _to_col_major(x):
    b, l, d = x.shape
    rows = l // GRID_W
    assert rows % SUB == 0
    return pl.pallas_call(
        functools.partial(_to_col_kernel, rows=rows, d=d),
        grid=(b, GRID_W // SUB),
        in_specs=[pl.BlockSpec((1, rows, SUB, d), lambda bi, wi: (bi, 0, wi, 0))],
        out_specs=pl.BlockSpec((1, SUB * rows, d), lambda bi, wi: (bi, wi, 0)),
        out_shape=jax.ShapeDtypeStruct((b, l, d), x.dtype),
        compiler_params=_params(("parallel", "arbitrary")),
        name="to_col_major",
    )(x.reshape(b, rows, GRID_W, d))


def _rms(xw, g):
    ms = jnp.mean(xw * xw, axis=-1, keepdims=True)
    return xw * lax.rsqrt(ms + EPS) * g


def _final_col_kernel(x_ref, g_ref, o_ref, *, rows, d):
    g = g_ref[...]
    slabs = [_rms(x_ref[0, w * rows:(w + 1) * rows, :], g).reshape(rows // SUB, SUB, d) for w in range(SUB)]
    res = _swap_slab_sublane(slabs)
    o_ref[0] = jnp.stack(res, axis=1).reshape(rows, SUB, d)


def _final_row_kernel(x_ref, g_ref, o_ref):
    o_ref[0] = _rms(x_ref[0], g_ref[...])


def _final_norm(x, g, col_major):
    b, l, d = x.shape
    g2 = g.reshape(1, d)
    if not col_major:
        t_rows = min(ROW_TILE, l)
        tile = pl.BlockSpec((1, t_rows, d), lambda bi, ti: (bi, ti, 0))
        return pl.pallas_call(
            _final_row_kernel, grid=(b, l // t_rows), in_specs=[tile, _full(g2)], out_specs=tile,
            out_shape=jax.ShapeDtypeStruct((b, l, d), F32),
            compiler_params=_params(("parallel", "arbitrary")),
            name="final_norm",
        )(x, g2)
    rows = l // GRID_W
    assert rows % SUB == 0
    out = pl.pallas_call(
        functools.partial(_final_col_kernel, rows=rows, d=d),
        grid=(b, GRID_W // SUB),
        in_specs=[pl.BlockSpec((1, SUB * rows, d), lambda bi, wi: (bi, wi, 0)), _full(g2)],
        out_specs=pl.BlockSpec((1, rows, SUB, d), lambda bi, wi: (bi, 0, wi, 0)),
        out_shape=jax.ShapeDtypeStruct((b, rows, GRID_W, d), F32),
        compiler_params=_params(("parallel", "arbitrary")),
        name="final_norm_col",
    )(x, g2)
    return out.reshape(b, l, d)


def kernel(x, c, ctx, c_ctx, mod_w, mod_b, norm1_g, norm2_g, final_g, rg_w_in, rg_conv_w, rg_conv_b, rg_w_a, rg_b_a, rg_w_i, rg_b_i, rg_lam, rg_w_out, hy_w_in, hy_short_w, hy_short_b, hy_pe_w1, hy_pe_b1, hy_pe_w2, hy_pe_b2, hy_pe_w3, hy_pe_b3, hy_pe_w4, hy_freq, hy_skip, hy_w_out, ffn_w_up, ffn_conv_w, ffn_conv_b, ffn_w_down):
    b, l, d = x.shape
    lc = ctx.shape[1]
    depth = mod_w.shape[0]
    n_mixers = 2
    ctx_needed = [any((q % n_mixers) == 0 for q in range(i + 1, depth)) for i in range(depth)]

    bp = -(-(b + 1) // HALO) * HALO
    cpad = jnp.zeros((bp, d), F32).at[:b].set(c).at[b].set(c_ctx)
    mods = _mod_call(cpad, mod_w, mod_b)

    dft = {}
    h = x
    s = ctx
    col = False
    zeros_state = jnp.zeros((b, d), F32)
    for i in range(depth):
        kind = i % n_mixers
        j = i // n_mixers
        col_major = j % 2 == 1
        keep_ctx = ctx_needed[i]
        if col_major != col:
            assert col_major, "column-major layers are expected to be contiguous at the end"
            h = _to_col_major(h)
            col = True
        lat = [mods[i, :b, q * d:(q + 1) * d].reshape(b, 1, d) for q in range(6)]
        cmod = [jnp.broadcast_to(mods[i, b, q * d:(q + 1) * d].reshape(1, 1, d), (b, 1, d)) for q in range(6)]
        ng1 = norm1_g[i].reshape(1, d)
        ng2 = norm2_g[i].reshape(1, d)

        if kind == 0:
            w = (ng1, rg_w_in[j].astype(BF16), rg_conv_w[j], rg_conv_b[j].reshape(1, d),
                 rg_w_a[j].astype(BF16), rg_b_a[j].reshape(2, 1, d), rg_w_i[j].astype(BF16),
                 rg_b_i[j].reshape(2, 1, d), rg_lam[j].reshape(2, 1, d), rg_w_out[j].astype(BF16))
            s_new, st_f, st_b = _rg_layer(s, cmod[0], cmod[1], cmod[2], *w, zeros_state, zeros_state)
            h, _, _ = _rg_layer(h, lat[0], lat[1], lat[2], *w, st_f, st_b)
            if keep_ctx:
                s = s_new
        else:
            win = hy_w_in[j].astype(BF16)
            cw = hy_short_w[j]
            cb = hy_short_b[j].reshape(1, 3 * d)
            wout = hy_w_out[j].astype(BF16)
            skip = hy_skip[j].reshape(1, d)
            pe = (hy_pe_w1[j], hy_pe_b1[j], hy_pe_w2[j], hy_pe_b2[j], hy_pe_w3[j], hy_pe_b3[j], hy_pe_w4[j],
                  hy_freq[j])
            streams = [(h, lat, True)] + ([(s, cmod, False)] if keep_ctx else [])
            for stream, md, is_lat in streams:
                n = stream.shape[1]
                two_level = n % FFT_P == 0 and n // FFT_P >= 2
                if n not in dft:
                    dft[n] = _fft_mats(n) if two_level else _dft_mats(n)
                kf = _hy_filter_call(n, *pe)
                x0, xv = _hy_in_call(stream, md[0], md[1], ng1, win, cw, cb)
                if two_level:
                    w1, w1i, w2, w2i = dft[n]
                    pv = _hy_conv2_call(dft[n], xv, x0, _hy_kspec2_call(w1, w2, kf), skip)
                    new = _hy_out_call(pv, stream, md[2], wout)
                else:
                    cm, sm = dft[n]
                    kr, ki = _hy_spec_call(cm, sm, kf)
                    new = _hy_conv_call(cm, sm, xv, kr, ki, x0, stream, skip, md[2], wout)
                if is_lat:
                    h = new
                else:
                    s = new

        ffn = (ng2, ffn_w_up[i].astype(BF16), ffn_conv_w[i], ffn_conv_b[i].reshape(1, -1), ffn_w_down[i].astype(BF16))
        h = _ffn_call(h, lat[3], lat[4], lat[5], *ffn)
        if keep_ctx:
            s = _ffn_call(s, cmod[3], cmod[4], cmod[5], *ffn)
    return _final_norm(h, final_g, col)
```

```python
import functools
import math

import jax
import jax.numpy as jnp
from jax import lax
from jax.experimental import pallas as pl
from jax.experimental.pallas import tpu as pltpu

F32 = jnp.float32
BF16 = jnp.bfloat16

GRID_W = 64
RG_C = 8.0
HY_FAST_DECAY = 0.3
HY_SLOW_DECAY = 1.5
HY_TARGET = 1e-2
EPS = 1e-6
TINY = 1e-30
LOG2E = 1.4426950408889634

SUB = 8
LANE = 128
HALO = SUB
ROW_TILE = 512
RG_TT = 64
DFT_FB = 256
COL_CHUNK = 256
FFT_P = 256
FFT_R = 16
FFT_UNROLL = 8
FFT_ROW_UNROLL = 4
VMEM_LIMIT = 56 * 1024 * 1024


def _pick(n, cap, mult):
    best = None
    for d in range(mult, min(n, cap) + 1, mult):
        if n % d == 0:
            best = d
    return n if best is None else best


def _full(arr, single=True):
    nd = arr.ndim
    if single:
        return pl.BlockSpec(arr.shape, lambda *_: (0,) * nd, pipeline_mode=pl.Buffered(1))
    return pl.BlockSpec(arr.shape, lambda *_: (0,) * nd)


def _params(sem):
    return pltpu.CompilerParams(dimension_semantics=sem, vmem_limit_bytes=VMEM_LIMIT)


def _sigmoid(x):
    return 0.5 + 0.5 * jnp.tanh(0.5 * x)


def _gelu_tanh(x):
    return 0.5 * x * (1.0 + jnp.tanh(math.sqrt(2.0 / math.pi) * (x + 0.044715 * (x * x * x))))


def _norm_mod(x, ng, sc, sh):
    ms = jnp.mean(x * x, axis=-1, keepdims=True)
    y = x * lax.rsqrt(ms + EPS)
    return (y * ng) * (1.0 + sc) + sh


def _shift_rows(z, k):
    if k == 0:
        return z
    return pltpu.roll(z, (-k) % z.shape[0], 0)


def _conv_rows(z, cw, cb, offsets, t):
    acc = None
    for i, o in enumerate(offsets):
        term = cw[i:i + 1, :] * _shift_rows(z, o)[HALO:HALO + t]
        acc = term if acc is None else acc + term
    return acc + cb


def _halo_specs(t_rows, l, d):
    r = t_rows // HALO
    last = l // HALO - 1

    def prev_map(b, t):
        return (b, jnp.maximum(t * r - 1, 0), 0)

    def next_map(b, t):
        return (b, jnp.minimum((t + 1) * r, last), 0)

    return (pl.BlockSpec((1, HALO, d), prev_map),
            pl.BlockSpec((1, t_rows, d), lambda b, t: (b, t, 0)),
            pl.BlockSpec((1, HALO, d), next_map))


def _ext_rows(hp_ref, h_ref, hn_ref, ng, sc, sh):
    t = pl.program_id(1)
    nt = pl.num_programs(1)
    x = h_ref[0]
    xm = _norm_mod(x, ng, sc, sh)
    xp = jnp.where(t > 0, _norm_mod(hp_ref[0], ng, sc, sh), 0.0)
    xn = jnp.where(t < nt - 1, _norm_mod(hn_ref[0], ng, sc, sh), 0.0)
    return x, jnp.concatenate([xp, xm, xn], axis=0).astype(BF16)


def _mod_kernel(c_ref, w_ref, b_ref, o_ref):
    cv = c_ref[...]
    ca = cv * _sigmoid(cv)
    o_ref[0] = jnp.dot(ca.astype(BF16), w_ref[0].astype(BF16), preferred_element_type=F32) + b_ref[0]


def _mod_call(cpad, mod_w, mod_b):
    depth, d, m6 = mod_w.shape
    bp = cpad.shape[0]
    tn = _pick(m6, 1536, LANE)
    return pl.pallas_call(
        _mod_kernel,
        grid=(depth, m6 // tn),
        in_specs=[pl.BlockSpec((bp, d), lambda i, j: (0, 0)),
                  pl.BlockSpec((1, d, tn), lambda i, j: (i, 0, j)),
                  pl.BlockSpec((1, 1, tn), lambda i, j: (i, 0, j))],
        out_specs=pl.BlockSpec((1, bp, tn), lambda i, j: (i, 0, j)),
        out_shape=jax.ShapeDtypeStruct((depth, bp, m6), F32),
        compiler_params=_params(("arbitrary", "arbitrary")),
        name="modulation",
    )(cpad, mod_w, mod_b.reshape(depth, 1, m6))


def _ffn_kernel(hp_ref, h_ref, hn_ref, sh_ref, sc_ref, g_ref, ng_ref, wup_ref, cw_ref, cb_ref, wdn_ref,
                o_ref, act_s, *, t_rows, f, ck):
    x, xe = _ext_rows(hp_ref, h_ref, hn_ref, ng_ref[...], sc_ref[0], sh_ref[0])
    offs = (-1, 0, 1)
    for ci in range(f // ck):
        lo = ci * ck
        zg = jnp.dot(xe, wup_ref[:, lo:lo + ck], preferred_element_type=F32)
        zu = jnp.dot(xe, wup_ref[:, f + lo:f + lo + ck], preferred_element_type=F32)
        g = _conv_rows(zg, cw_ref[:, lo:lo + ck], cb_ref[:, lo:lo + ck], offs, t_rows)
        u = _conv_rows(zu, cw_ref[:, f + lo:f + lo + ck], cb_ref[:, f + lo:f + lo + ck], offs, t_rows)
        act_s[:, lo:lo + ck] = (g * _sigmoid(g) * u).astype(BF16)
    out = jnp.dot(act_s[...], wdn_ref[...], preferred_element_type=F32)
    o_ref[0] = x + g_ref[0] * out


def _ffn_call(h, sh, sc, g, ng, wup, cw, cb, wdn):
    b, l, d = h.shape
    f = wdn.shape[0]
    t_rows = min(ROW_TILE, l)
    ck = _pick(f, COL_CHUNK, LANE)
    vec = pl.BlockSpec((1, 1, d), lambda bi, ti: (bi, 0, 0))
    kern = functools.partial(_ffn_kernel, t_rows=t_rows, f=f, ck=ck)
    return pl.pallas_call(
        kern,
        grid=(b, l // t_rows),
        in_specs=[*_halo_specs(t_rows, l, d), vec, vec, vec, _full(ng), _full(wup), _full(cw), _full(cb), _full(wdn)],
        out_specs=pl.BlockSpec((1, t_rows, d), lambda bi, ti: (bi, ti, 0)),
        out_shape=jax.ShapeDtypeStruct((b, l, d), F32),
        scratch_shapes=[pltpu.VMEM((t_rows, f), BF16)],
        compiler_params=_params(("parallel", "arbitrary")),
        name="conv_ffn",
    )(h, h, h, sh, sc, g, ng, wup, cw, cb, wdn)


def _rg_gates(u2, wa_ref, ba_ref, wi_ref, bi_ref, lam_ref, heads):
    d = u2.shape[1]
    hd = d // heads
    ub = u2.astype(BF16)

    def block_diag(w_ref):
        return jnp.concatenate(
            [jnp.dot(ub[:, k * hd:(k + 1) * hd], w_ref[k], preferred_element_type=F32) for k in range(heads)],
            axis=1)

    ta = jnp.tanh(block_diag(wa_ref) + ba_ref[...])
    ti = jnp.tanh(block_diag(wi_ref) + bi_ref[...])
    lam = lam_ref[...]
    softplus_neg = jnp.maximum(-lam, 0.0) + jnp.log1p(jnp.exp(-jnp.abs(lam)))
    half = (0.5 * RG_C) * softplus_neg
    neg_log_a = half + half * ta
    half2 = (-LOG2E) * half
    a = jnp.exp2(half2 + half2 * ta)
    om = jnp.tanh(neg_log_a) * (1.0 + a * a)
    beta = om * lax.rsqrt(jnp.maximum(om, TINY))
    return a, (beta * (0.5 * u2)) * (1.0 + ti)


def _swap_slab_sublane(slabs):
    sub = lax.broadcasted_iota(jnp.int32, slabs[0].shape, 1)
    s = SUB // 2
    while s:
        bit = (sub & s) != 0
        new = list(slabs)
        for k in range(SUB):
            if k & s:
                continue
            lo, hi = slabs[k], slabs[k | s]
            new[k] = jnp.where(bit, pltpu.roll(hi, s, 1), lo)
            new[k | s] = jnp.where(bit, hi, pltpu.roll(lo, SUB - s, 1))
        slabs = new
        s //= 2
    return slabs


def _to_time_major(x):
    _, t, c = x.shape
    slabs = _swap_slab_sublane([x[k].reshape(t // SUB, SUB, c) for k in range(SUB)])
    return jnp.stack(slabs, axis=1).reshape(t * SUB, c)


def _to_batch_major(y, t):
    c = y.shape[1]
    y4 = y.reshape(t // SUB, SUB, SUB, c)
    slabs = _swap_slab_sublane([y4[:, k] for k in range(SUB)])
    return jnp.stack([s.reshape(t, c) for s in slabs], axis=0)


def _scan_tile(a_s, b_s, state_s, tt, reverse):
    def step(s, h):
        pos = (tt - 1 - s) if reverse else s
        rows = pl.ds(pl.multiple_of(pos * SUB, SUB), SUB)
        h = a_s[rows, :] * h + b_s[rows, :]
        b_s[rows, :] = h
        return h

    h = lax.fori_loop(0, tt, step, state_s[...], unroll=8)
    state_s[...] = h
    return h


def _rg_fwd_kernel(hp_ref, h_ref, hn_ref, sh_ref, sc_ref, ng_ref, win_ref, cw_ref, cb_ref,
                   wa_ref, ba_ref, wi_ref, bi_ref, lam_ref, h0_ref,
                   gate_ref, u_ref, hf_ref, st_ref, a_s, b_s, state_s, *, tt, d, heads):
    t = pl.program_id(1)
    nt = pl.num_programs(1)
    n = tt * SUB

    @pl.when(t == 0)
    def _():
        state_s[...] = h0_ref[...]

    ng = ng_ref[...]
    sc = sc_ref[0]
    sh = sh_ref[0]

    def nm(rows):
        return _norm_mod(rows.reshape(-1, SUB, d), ng, sc, sh).reshape(rows.shape)

    xm = nm(_to_time_major(h_ref[...]))
    xp = jnp.where(t > 0, nm(_to_time_major(hp_ref[...])[(HALO - 1) * SUB:, :]), 0.0)
    xn = jnp.where(t < nt - 1, nm(_to_time_major(hn_ref[...])[:3 * SUB, :]), 0.0)
    xe = jnp.concatenate([xp, xm, xn], axis=0).astype(BF16)

    gate = jnp.dot(xm.astype(BF16), win_ref[:, :d], preferred_element_type=F32)
    gate_ref[0] = gate.astype(gate_ref.dtype)

    ur = jnp.dot(xe, win_ref[:, d:], preferred_element_type=F32)
    cw = cw_ref[...]
    u = cb_ref[...]
    for i in range(cw.shape[0]):
        u = u + cw[i:i + 1, :] * ur[i * SUB:i * SUB + n]
    u_ref[0] = u

    a, b_in = _rg_gates(u, wa_ref, ba_ref, wi_ref, bi_ref, lam_ref, heads)
    a_s[...] = a
    b_s[...] = b_in
    st_ref[...] = _scan_tile(a_s, b_s, state_s, tt, reverse=False)
    hf_ref[0] = b_s[...]


def _rg_bwd_kernel(u_ref, gate_ref, hf_ref, h_ref, g1_ref, wa_ref, ba_ref, wi_ref, bi_ref, lam_ref, h0_ref,
                   wout_ref, o_ref, st_ref, a_s, b_s, state_s, *, tt, d, heads):
    t = pl.program_id(1)

    @pl.when(t == 0)
    def _():
        state_s[...] = h0_ref[...]

    a, b_in = _rg_gates(u_ref[0], wa_ref, ba_ref, wi_ref, bi_ref, lam_ref, heads)
    a_s[...] = a
    b_s[...] = b_in
    st_ref[...] = _scan_tile(a_s, b_s, state_s, tt, reverse=True)

    hsum = hf_ref[0] + b_s[...]
    y = (hsum * _gelu_tanh(gate_ref[0].astype(F32))).astype(BF16)
    out = jnp.dot(y, wout_ref[...], preferred_element_type=F32)
    out = (out.reshape(tt, SUB, d) * g1_ref[0]).reshape(tt * SUB, d)
    o_ref[...] = h_ref[...] + _to_batch_major(out, tt)


def _rg_layer(h, sh, sc, g1, ng, win, cw, cb, wa, ba, wi, bi, lam, wout, h0_f, h0_b):
    b, l, d = h.shape
    heads = wa.shape[1]
    assert b % SUB == 0 and l % HALO == 0, "the recurrent kernels scan SUB batch rows per sublane tile"
    groups = b // SUB
    tt = min(RG_TT, l)
    nt = l // tt
    r = tt // HALO
    n = tt * SUB
    grid = (groups, nt)
    sem = _params(("parallel", "arbitrary"))
    sh, sc, g1 = (v.reshape(groups, SUB, d) for v in (sh, sc, g1))

    vec = pl.BlockSpec((1, SUB, d), lambda gi, ti: (gi, 0, 0))
    st_spec = pl.BlockSpec((SUB, d), lambda gi, ti: (gi, 0))
    tile_f = pl.BlockSpec((SUB, tt, d), lambda gi, ti: (gi, ti, 0))
    tm_f = pl.BlockSpec((1, n, d), lambda gi, ti: (gi, ti, 0))
    prev_spec = pl.BlockSpec((SUB, HALO, d), lambda gi, ti: (gi, jnp.maximum(ti * r - 1, 0), 0))
    next_spec = pl.BlockSpec((SUB, HALO, d), lambda gi, ti: (gi, jnp.minimum((ti + 1) * r, nt * r - 1), 0))
    scratch = [pltpu.VMEM((n, d), F32), pltpu.VMEM((n, d), F32), pltpu.VMEM((SUB, d), F32)]
    tm_shape = (groups, l * SUB, d)

    def dirw(k):
        return wa[k], ba[k], wi[k], bi[k], lam[k]

    fw = dirw(0)
    gate, u, hf, st_f = pl.pallas_call(
        functools.partial(_rg_fwd_kernel, tt=tt, d=d, heads=heads),
        grid=grid,
        in_specs=[prev_spec, tile_f, next_spec, vec, vec, _full(ng), _full(win), _full(cw), _full(cb),
                  *[_full(w) for w in fw], st_spec],
        out_specs=[tm_f, tm_f, tm_f, st_spec],
        out_shape=[jax.ShapeDtypeStruct(tm_shape, BF16), jax.ShapeDtypeStruct(tm_shape, F32),
                   jax.ShapeDtypeStruct(tm_shape, F32), jax.ShapeDtypeStruct((b, d), F32)],
        scratch_shapes=scratch,
        compiler_params=sem,
        name="rg_fwd",
    )(h, h, h, sh, sc, ng, win, cw, cb, *fw, h0_f)

    tile_r = pl.BlockSpec((SUB, tt, d), lambda gi, ti: (gi, nt - 1 - ti, 0))
    tm_r = pl.BlockSpec((1, n, d), lambda gi, ti: (gi, nt - 1 - ti, 0))
    bw = dirw(1)
    h_new, st_b = pl.pallas_call(
        functools.partial(_rg_bwd_kernel, tt=tt, d=d, heads=heads),
        grid=grid,
        in_specs=[tm_r, tm_r, tm_r, tile_r, vec, *[_full(w) for w in bw], st_spec, _full(wout)],
        out_specs=[tile_r, st_spec],
        out_shape=[jax.ShapeDtypeStruct((b, l, d), F32), jax.ShapeDtypeStruct((b, d), F32)],
        scratch_shapes=scratch,
        compiler_params=sem,
        name="rg_bwd",
    )(u, gate, hf, h, g1, *bw, h0_b, wout)
    return h_new, st_f, st_b


def _hy_in_kernel(hp_ref, h_ref, hn_ref, sh_ref, sc_ref, ng_ref, win_ref, cw_ref, cb_ref,
                  x0_ref, xv_ref, *, t_rows, d, ck):
    _, xe = _ext_rows(hp_ref, h_ref, hn_ref, ng_ref[...], sc_ref[0], sh_ref[0])
    offs = (-1, 0, 1)

    def proj(lo):
        z = jnp.dot(xe, win_ref[:, lo:lo + ck], preferred_element_type=F32)
        return _conv_rows(z, cw_ref[:, lo:lo + ck], cb_ref[:, lo:lo + ck], offs, t_rows)

    for ci in range(d // ck):
        lo = ci * ck
        x0_ref[0, :, lo:lo + ck] = proj(lo).astype(BF16)
        xv_ref[0, :, lo:lo + ck] = (proj(d + lo) * proj(2 * d + lo)).astype(BF16)


def _hy_in_call(h, sh, sc, ng, win, cw, cb):
    b, l, d = h.shape
    t_rows = min(ROW_TILE, l)
    ck = _pick(d, COL_CHUNK, LANE)
    vec = pl.BlockSpec((1, 1, d), lambda bi, ti: (bi, 0, 0))
    tile = pl.BlockSpec((1, t_rows, d), lambda bi, ti: (bi, ti, 0))
    return pl.pallas_call(
        functools.partial(_hy_in_kernel, t_rows=t_rows, d=d, ck=ck),
        grid=(b, l // t_rows),
        in_specs=[*_halo_specs(t_rows, l, d), vec, vec, _full(ng), _full(win), _full(cw), _full(cb)],
        out_specs=[tile, tile],
        out_shape=[jax.ShapeDtypeStruct((b, l, d), BF16), jax.ShapeDtypeStruct((b, l, d), BF16)],
        compiler_params=_params(("parallel", "arbitrary")),
        name="hy_in",
    )(h, h, h, sh, sc, ng, win, cw, cb)


def _hy_filter_kernel(w1_ref, b1_ref, w2_ref, b2_ref, w3_ref, b3_ref, w4_ref, fr_ref, k_ref, hdn_s,
                      *, n, d, dc, bands):
    j = pl.program_id(0)
    hp = lax.Precision.HIGHEST

    @pl.when(j == 0)
    def _():
        s = lax.broadcasted_iota(jnp.int32, (n, LANE), 0).astype(F32)
        lane = lax.broadcasted_iota(jnp.int32, (n, LANE), 1)
        tpos = s / float(n - 1)
        w = s * (2.0 * math.pi / n)
        bidx = jnp.where(lane <= bands, lane - 1, lane - 1 - bands).astype(F32)
        band = 1e-4 + bidx * ((bands - 1 - 1e-4) / (bands - 1))
        arg = band * w
        z = jnp.where(lane == 0, tpos,
                      jnp.where(lane <= bands, jnp.cos(arg),
                                jnp.where(lane <= 2 * bands, -jnp.sin(arg), 0.0)))
        fr = fr_ref[...]
        hd = jnp.sin(fr * (jnp.dot(z, w1_ref[...], precision=hp, preferred_element_type=F32) + b1_ref[...]))
        hd = jnp.sin(fr * (jnp.dot(hd, w2_ref[...], precision=hp, preferred_element_type=F32) + b2_ref[...]))
        hd = jnp.sin(fr * (jnp.dot(hd, w3_ref[...], precision=hp, preferred_element_type=F32) + b3_ref[...]))
        hdn_s[...] = hd

    k = jnp.dot(hdn_s[...], w4_ref[...], precision=hp, preferred_element_type=F32)
    srow = lax.broadcasted_iota(jnp.int32, (n, 1), 0).astype(F32)
    centre = n // 2
    dist = jnp.abs(srow - float(centre)) / float(centre)
    ch = (j * dc + lax.broadcasted_iota(jnp.int32, (1, dc), 1)).astype(F32)
    d_lo = math.log(HY_TARGET) / HY_SLOW_DECAY
    d_hi = math.log(HY_TARGET) / HY_FAST_DECAY
    delta = d_lo + ch * ((d_hi - d_lo) / (d - 1))
    k = k * jnp.exp(-dist * jnp.abs(delta))
    k = k / jnp.sum(jnp.abs(k), axis=0, keepdims=True)
    k_ref[...] = k.astype(k_ref.dtype)


def _hy_filter_call(n, w1, b1, w2, b2, w3, b3, w4, fr):
    emb, fd = w1.shape
    d = w4.shape[1]
    bands = (emb - 1) // 2
    w1p = jnp.zeros((LANE, fd), F32).at[:emb].set(w1)
    dc = _pick(d, COL_CHUNK, LANE)
    args = (w1p, b1.reshape(1, fd), w2, b2.reshape(1, fd), w3, b3.reshape(1, fd))
    return pl.pallas_call(
        functools.partial(_hy_filter_kernel, n=n, d=d, dc=dc, bands=bands),
        grid=(d // dc,),
        in_specs=[*[_full(a, single=False) for a in args],
                  pl.BlockSpec((fd, dc), lambda j: (0, j)), _full(fr.reshape(1, fd), single=False)],
        out_specs=pl.BlockSpec((n, dc), lambda j: (0, j)),
        out_shape=jax.ShapeDtypeStruct((n, d), BF16),
        scratch_shapes=[pltpu.VMEM((n, fd), F32)],
        compiler_params=_params(("arbitrary",)),
        name="hy_filter",
    )(*args, w4, fr.reshape(1, fd))


def _dft_mats(n):
    i = 2 * jnp.arange(n, dtype=jnp.int32) + 1
    m = (i[:, None] * i[None, :]) % (8 * n)
    ang = m.astype(F32) * (2.0 * math.pi / (8 * n))
    return jnp.cos(ang).astype(BF16), jnp.sin(ang).astype(BF16)


def _hy_spec_kernel(c_ref, s_ref, k_ref, kr_ref, ki_ref, *, n, fb):
    j = pl.program_id(0)
    kk = k_ref[...]
    a = jnp.dot(c_ref[...], kk, preferred_element_type=F32)
    b = jnp.dot(s_ref[...], kk, preferred_element_type=F32)
    fi = 2 * (j * fb + lax.broadcasted_iota(jnp.int32, (fb, 1), 0)) + 1
    m = lax.rem(fi * (n + 1), 8 * n)
    ang = m.astype(F32) * (2.0 * math.pi / (8 * n))
    qr = jnp.cos(ang) * (1.0 / n)
    qi = jnp.sin(ang) * (1.0 / n)
    kr_ref[...] = qr * a + qi * b
    ki_ref[...] = qi * a - qr * b


def _hy_spec_call(cm, sm, k):
    n, d = k.shape
    fb = min(DFT_FB, n)
    row = pl.BlockSpec((fb, n), lambda j: (j, 0))
    out = pl.BlockSpec((fb, d), lambda j: (j, 0))
    return pl.pallas_call(
        functools.partial(_hy_spec_kernel, n=n, fb=fb),
        grid=(n // fb,),
        in_specs=[row, row, _full(k)],
        out_specs=[out, out],
        out_shape=[jax.ShapeDtypeStruct((n, d), F32), jax.ShapeDtypeStruct((n, d), F32)],
        compiler_params=_params(("arbitrary",)),
        name="hy_spec_dense",
    )(cm, sm, k)


def _hy_conv_kernel(c_ref, s_ref, xv_ref, kr_ref, ki_ref, x0_ref, h_ref, skip_ref, g1_ref, wout_ref,
                    o_ref, zr_s, zi_s, *, fb):
    p = pl.program_id(1)
    j = pl.program_id(2)
    rows = pl.ds(pl.multiple_of(j * fb, fb), fb)

    @pl.when(p == 0)
    def _():
        xv = xv_ref[0]
        xr = jnp.dot(c_ref[...], xv, preferred_element_type=F32)
        xi = -jnp.dot(s_ref[...], xv, preferred_element_type=F32)
        kr = kr_ref[...]
        ki = ki_ref[...]
        zr_s[rows, :] = (xr * kr - xi * ki).astype(BF16)
        zi_s[rows, :] = (xr * ki + xi * kr).astype(BF16)

    @pl.when(p == 1)
    def _():
        y = (jnp.dot(c_ref[...], zr_s[...], preferred_element_type=F32)
             - jnp.dot(s_ref[...], zi_s[...], preferred_element_type=F32))
        y = y + xv_ref[0, rows, :].astype(F32) * skip_ref[...]
        pv = (x0_ref[0].astype(F32) * y).astype(BF16)
        out = jnp.dot(pv, wout_ref[...], preferred_element_type=F32)
        o_ref[0] = h_ref[0] + g1_ref[0] * out


def _hy_conv_call(cm, sm, xv, kr, ki, x0, h, skip, g1, wout):
    b, n, d = h.shape
    fb = min(DFT_FB, n)
    nf = n // fb
    row = pl.BlockSpec((fb, n), lambda bi, p, j: (j, 0))
    kspec = pl.BlockSpec((fb, d), lambda bi, p, j: (jnp.where(p == 0, j, nf - 1), 0))
    tile = pl.BlockSpec((1, fb, d), lambda bi, p, j: (bi, jnp.where(p == 1, j, 0), 0))
    return pl.pallas_call(
        functools.partial(_hy_conv_kernel, fb=fb),
        grid=(b, 2, nf),
        in_specs=[row, row,
                  pl.BlockSpec((1, n, d), lambda bi, p, j: (bi, 0, 0), pipeline_mode=pl.Buffered(1)),
                  kspec, kspec, tile, tile, _full(skip),
                  pl.BlockSpec((1, 1, d), lambda bi, p, j: (bi, 0, 0)), _full(wout)],
        out_specs=tile,
        out_shape=jax.ShapeDtypeStruct((b, n, d), F32),
        scratch_shapes=[pltpu.VMEM((n, d), BF16), pltpu.VMEM((n, d), BF16)],
        compiler_params=_params(("parallel", "arbitrary", "arbitrary")),
        name="hy_conv_dense",
    )(cm, sm, xv, kr, ki, x0, h, skip, g1, wout)


def _fft_mats(n):
    p, r = FFT_P, FFT_R
    nn = 2 * n
    q, qh, hh = nn // p, n // p, p // r
    i32 = jnp.int32
    t1h = jnp.arange(hh, dtype=i32).reshape(hh, 1, 1, 1)
    f2 = jnp.arange(q, dtype=i32).reshape(1, q, 1, 1)
    t1l = jnp.arange(r, dtype=i32).reshape(1, 1, r, 1)
    t2 = jnp.arange(qh, dtype=i32).reshape(1, 1, 1, qh)
    m = ((2 * f2 + 1) * (p * t2 + r * t1h + t1l)) % (2 * nn)
    th = m.astype(F32) * (math.pi / nn)
    base = jnp.stack([jnp.cos(th), -jnp.sin(th)], axis=1).astype(BF16)
    eye = jnp.eye(r, dtype=BF16)
    w1 = base[..., None] * eye[None, None, None, :, None, :]
    w1f = w1.reshape(hh, 2 * q * r, qh * r)
    w1i = jnp.transpose(w1, (0, 4, 5, 1, 2, 3)).reshape(hh, qh * r, 2 * q * r)
    f1 = jnp.arange(p // 2, dtype=i32)[:, None]
    t1 = jnp.arange(p, dtype=i32)[None, :]
    phi = ((f1 * t1) % p).astype(F32) * (2.0 * math.pi / p)
    c, sn = jnp.cos(phi), jnp.sin(phi)
    w2 = jnp.block([[c, sn], [-sn, c]]).astype(BF16)
    w2i = jnp.block([[c.T, -sn.T], [sn.T, c.T]]).astype(BF16)
    return w1f, w1i, w2, w2i


def _fft_coarse_fwd(load_rows, w1_ref, s_ref, n):
    p, r = FFT_P, FFT_R
    q, qh, hh = 2 * n // p, n // p, p // r

    def body(t1h, carry):
        u = jnp.concatenate([load_rows(pl.multiple_of(p * t2 + t1h * r, r)) for t2 in range(qh)], axis=0)
        res = jnp.dot(w1_ref[t1h], u, preferred_element_type=F32)
        for pf in range(2 * q):
            s_ref[pf * hh + t1h] = res[pf * r:(pf + 1) * r].astype(BF16)
        return carry

    lax.fori_loop(0, hh, body, 0, unroll=FFT_ROW_UNROLL)


def _fft_fine_fwd(s_ref, w2_ref, f2, n):
    p, r = FFT_P, FFT_R
    q, hh = 2 * n // p, p // r
    cc = s_ref.shape[-1]
    v = jnp.concatenate([s_ref[pl.ds(f2 * hh, hh)].reshape(p, cc),
                         s_ref[pl.ds((q + f2) * hh, hh)].reshape(p, cc)], axis=0)
    return jnp.dot(w2_ref[...], v, preferred_element_type=F32)


def _hy_kspec2_kernel(w1_ref, w2_ref, k_ref, o_ref, s_ref, *, n):
    p = FFT_P
    q = 2 * n // p
    _fft_coarse_fwd(lambda r0: k_ref[pl.ds(r0, FFT_R), :], w1_ref, s_ref, n)

    def body(f2, carry):
        xs = _fft_fine_fwd(s_ref, w2_ref, f2, n)
        xr, xi = xs[:p // 2], xs[p // 2:]
        f1 = lax.broadcasted_iota(jnp.int32, (p // 2, 1), 0)
        m = lax.rem(2 * (q * f1 + f2) + 1, 8)
        ang = m.astype(F32) * (math.pi / 4.0)
        qr = jnp.cos(ang) * (1.0 / n)
        qi = jnp.sin(ang) * (1.0 / n)
        o_ref[f2, 0] = qr * xr - qi * xi
        o_ref[f2, 1] = qr * xi + qi * xr
        return carry

    lax.fori_loop(0, q, body, 0)


def _hy_kspec2_call(w1, w2, k):
    n, d = k.shape
    p, r = FFT_P, FFT_R
    q, hh = 2 * n // p, p // r
    cc = _pick(d, COL_CHUNK, LANE)
    return pl.pallas_call(
        functools.partial(_hy_kspec2_kernel, n=n),
        grid=(d // cc,),
        in_specs=[_full(w1), _full(w2), pl.BlockSpec((n, cc), lambda c: (0, c))],
        out_specs=pl.BlockSpec((q, 2, p // 2, cc), lambda c: (0, 0, 0, c)),
        out_shape=jax.ShapeDtypeStruct((q, 2, p // 2, d), F32),
        scratch_shapes=[pltpu.VMEM((2 * q * hh, r, cc), BF16)],
        compiler_params=_params(("arbitrary",)),
        name="hy_kspec",
    )(w1, w2, k)


def _hy_conv2_kernel(w1_ref, w1i_ref, w2_ref, w2i_ref, xv_ref, x0_ref, k_ref, skip_ref, p_ref, s_ref, *, n):
    p, r = FFT_P, FFT_R
    q, qh, hh = 2 * n // p, n // p, p // r
    _fft_coarse_fwd(lambda r0: xv_ref[0, pl.ds(r0, r), :], w1_ref, s_ref, n)

    def mid(f2, carry):
        xs = _fft_fine_fwd(s_ref, w2_ref, f2, n)
        xr, xi = xs[:p // 2], xs[p // 2:]
        kr = k_ref[f2, 0]
        ki = k_ref[f2, 1]
        z = jnp.concatenate([xr * kr - xi * ki, xr * ki + xi * kr], axis=0).astype(BF16)
        c = jnp.dot(w2i_ref[...], z, preferred_element_type=F32)
        cc = c.shape[-1]
        s_ref[pl.ds(f2 * hh, hh)] = c[:p].reshape(hh, r, cc).astype(BF16)
        s_ref[pl.ds((q + f2) * hh, hh)] = c[p:].reshape(hh, r, cc).astype(BF16)
        return carry

    lax.fori_loop(0, q, mid, 0, unroll=FFT_UNROLL)
    skip = skip_ref[...]

    def back(t1h, carry):
        g = jnp.concatenate([s_ref[pf * hh + t1h] for pf in range(2 * q)], axis=0)
        y = jnp.dot(w1i_ref[t1h], g, preferred_element_type=F32)
        for t2 in range(qh):
            rows = pl.ds(pl.multiple_of(p * t2 + t1h * r, r), r)
            yv = y[t2 * r:(t2 + 1) * r] + xv_ref[0, rows, :].astype(F32) * skip
            p_ref[0, rows, :] = (x0_ref[0, rows, :].astype(F32) * yv).astype(p_ref.dtype)
        return carry

    lax.fori_loop(0, hh, back, 0, unroll=FFT_ROW_UNROLL)


def _hy_conv2_call(mats, xv, x0, kspec, skip):
    b, n, d = xv.shape
    p, r = FFT_P, FFT_R
    q, hh = 2 * n // p, p // r
    cc = _pick(d, COL_CHUNK, LANE)
    w1, w1i, w2, w2i = mats
    col = pl.BlockSpec((1, n, cc), lambda c, bi: (bi, 0, c))
    return pl.pallas_call(
        functools.partial(_hy_conv2_kernel, n=n),
        grid=(d // cc, b),
        in_specs=[_full(w1), _full(w1i), _full(w2), _full(w2i), col, col,
                  pl.BlockSpec((q, 2, p // 2, cc), lambda c, bi: (0, 0, 0, c), pipeline_mode=pl.Buffered(1)),
                  pl.BlockSpec((1, cc), lambda c, bi: (0, c))],
        out_specs=col,
        out_shape=jax.ShapeDtypeStruct((b, n, d), BF16),
        scratch_shapes=[pltpu.VMEM((2 * q * hh, r, cc), BF16)],
        compiler_params=_params(("arbitrary", "arbitrary")),
        name="hy_conv2",
    )(w1, w1i, w2, w2i, xv, x0, kspec, skip)


def _hy_out_kernel(p_ref, h_ref, g1_ref, wout_ref, o_ref):
    o_ref[0] = h_ref[0] + g1_ref[0] * jnp.dot(p_ref[0], wout_ref[...], preferred_element_type=F32)


def _hy_out_call(pv, h, g1, wout):
    b, l, d = h.shape
    t_rows = min(2 * ROW_TILE, l)
    tile = pl.BlockSpec((1, t_rows, d), lambda bi, ti: (bi, ti, 0))
    return pl.pallas_call(
        _hy_out_kernel,
        grid=(b, l // t_rows),
        in_specs=[tile, tile, pl.BlockSpec((1, 1, d), lambda bi, ti: (bi, 0, 0)), _full(wout)],
        out_specs=tile,
        out_shape=jax.ShapeDtypeStruct((b, l, d), F32),
        compiler_params=_params(("parallel", "arbitrary")),
        name="hy_out",
    )(pv, h, g1, wout)


def _to_col_kernel(x_ref, o_ref, *, rows, d):
    x4 = x_ref[0].reshape(rows // SUB, SUB, SUB, d)
    res = _swap_slab_sublane([x4[:, k] for k in range(SUB)])
    for w in range(SUB):
        o_ref[0, w * rows:(w + 1) * rows, :] = res[w].reshape(rows, d)


def _to_col_major(x):
    b, l, d = x.shape
    rows = l // GRID_W
    assert rows % SUB == 0
    return pl.pallas_call(
        functools.partial(_to_col_kernel, rows=rows, d=d),
        grid=(b, GRID_W // SUB),
        in_specs=[pl.BlockSpec((1, rows, SUB, d), lambda bi, wi: (bi, 0, wi, 0))],
        out_specs=pl.BlockSpec((1, SUB * rows, d), lambda bi, wi: (bi, wi, 0)),
        out_shape=jax.ShapeDtypeStruct((b, l, d), x.dtype),
        compiler_params=_params(("parallel", "arbitrary")),
        name="to_col_major",
    )(x.reshape(b, rows, GRID_W, d))


def _rms(xw, g):
    ms = jnp.mean(xw * xw, axis=-1, keepdims=True)
    return xw * lax.rsqrt(ms + EPS) * g


def _final_col_kernel(x_ref, g_ref, o_ref, *, rows, d):
    g = g_ref[...]
    slabs = [_rms(x_ref[0, w * rows:(w + 1) * rows, :], g).reshape(rows // SUB, SUB, d) for w in range(SUB)]
    res = _swap_slab_sublane(slabs)
    o_ref[0] = jnp.stack(res, axis=1).reshape(rows, SUB, d)


def _final_row_kernel(x_ref, g_ref, o_ref):
    o_ref[0] = _rms(x_ref[0], g_ref[...])


def _final_norm(x, g, col_major):
    b, l, d = x.shape
    g2 = g.reshape(1, d)
    if not col_major:
        t_rows = min(ROW_TILE, l)
        tile = pl.BlockSpec((1, t_rows, d), lambda bi, ti: (bi, ti, 0))
        return pl.pallas_call(
            _final_row_kernel, grid=(b, l // t_rows), in_specs=[tile, _full(g2)], out_specs=tile,
            out_shape=jax.ShapeDtypeStruct((b, l, d), F32),
            compiler_params=_params(("parallel", "arbitrary")),
            name="final_norm",
        )(x, g2)
    rows = l // GRID_W
    assert rows % SUB == 0
    out = pl.pallas_call(
        functools.partial(_final_col_kernel, rows=rows, d=d),
        grid=(b, GRID_W // SUB),
        in_specs=[pl.BlockSpec((1, SUB * rows, d), lambda bi, wi: (bi, wi, 0)), _full(g2)],
        out_specs=pl.BlockSpec((1, rows, SUB, d), lambda bi, wi: (bi, 0, wi, 0)),
        out_shape=jax.ShapeDtypeStruct((b, rows, GRID_W, d), F32),
        compiler_params=_params(("parallel", "arbitrary")),
        name="final_norm_col",
    )(x, g2)
    return out.reshape(b, l, d)


def kernel(x, c, ctx, c_ctx, mod_w, mod_b, norm1_g, norm2_g, final_g, rg_w_in, rg_conv_w, rg_conv_b, rg_w_a, rg_b_a, rg_w_i, rg_b_i, rg_lam, rg_w_out, hy_w_in, hy_short_w, hy_short_b, hy_pe_w1, hy_pe_b1, hy_pe_w2, hy_pe_b2, hy_pe_w3, hy_pe_b3, hy_pe_w4, hy_freq, hy_skip, hy_w_out, ffn_w_up, ffn_conv_w, ffn_conv_b, ffn_w_down):
    b, l, d = x.shape
    lc = ctx.shape[1]
    depth = mod_w.shape[0]
    n_mixers = 2
    ctx_needed = [any((q % n_mixers) == 0 for q in range(i + 1, depth)) for i in range(depth)]

    bp = -(-(b + 1) // HALO) * HALO
    cpad = jnp.zeros((bp, d), F32).at[:b].set(c).at[b].set(c_ctx)
    mods = _mod_call(cpad, mod_w, mod_b)

    dft = {}
    h = x
    s = ctx
    col = False
    zeros_state = jnp.zeros((b, d), F32)
    for i in range(depth):
        kind = i % n_mixers
        j = i // n_mixers
        col_major = j % 2 == 1
        keep_ctx = ctx_needed[i]
        if col_major != col:
            assert col_major, "column-major layers are expected to be contiguous at the end"
            h = _to_col_major(h)
            col = True
        lat = [mods[i, :b, q * d:(q + 1) * d].reshape(b, 1, d) for q in range(6)]
        cmod = [jnp.broadcast_to(mods[i, b, q * d:(q + 1) * d].reshape(1, 1, d), (b, 1, d)) for q in range(6)]
        ng1 = norm1_g[i].reshape(1, d)
        ng2 = norm2_g[i].reshape(1, d)

        if kind == 0:
            w = (ng1, rg_w_in[j].astype(BF16), rg_conv_w[j], rg_conv_b[j].reshape(1, d),
                 (0.5 * rg_w_a[j]).astype(BF16), 0.5 * rg_b_a[j].reshape(2, 1, d), (0.5 * rg_w_i[j]).astype(BF16),
                 0.5 * rg_b_i[j].reshape(2, 1, d), rg_lam[j].reshape(2, 1, d), rg_w_out[j].astype(BF16))
            s_new, st_f, st_b = _rg_layer(s, cmod[0], cmod[1], cmod[2], *w, zeros_state, zeros_state)
            h, _, _ = _rg_layer(h, lat[0], lat[1], lat[2], *w, st_f, st_b)
            if keep_ctx:
                s = s_new
        else:
            win = hy_w_in[j].astype(BF16)
            cw = hy_short_w[j]
            cb = hy_short_b[j].reshape(1, 3 * d)
            wout = hy_w_out[j].astype(BF16)
            skip = hy_skip[j].reshape(1, d)
            pe = (hy_pe_w1[j], hy_pe_b1[j], hy_pe_w2[j], hy_pe_b2[j], hy_pe_w3[j], hy_pe_b3[j], hy_pe_w4[j],
                  hy_freq[j])
            streams = [(h, lat, True)] + ([(s, cmod, False)] if keep_ctx else [])
            for stream, md, is_lat in streams:
                n = stream.shape[1]
                two_level = n % FFT_P == 0 and n // FFT_P >= 2
                if n not in dft:
                    dft[n] = _fft_mats(n) if two_level else _dft_mats(n)
                kf = _hy_filter_call(n, *pe)
                x0, xv = _hy_in_call(stream, md[0], md[1], ng1, win, cw, cb)
                if two_level:
                    w1, w1i, w2, w2i = dft[n]
                    pv = _hy_conv2_call(dft[n], xv, x0, _hy_kspec2_call(w1, w2, kf), skip)
                    new = _hy_out_call(pv, stream, md[2], wout)
                else:
                    cm, sm = dft[n]
                    kr, ki = _hy_spec_call(cm, sm, kf)
                    new = _hy_conv_call(cm, sm, xv, kr, ki, x0, stream, skip, md[2], wout)
                if is_lat:
                    h = new
                else:
                    s = new

        ffn = (ng2, ffn_w_up[i].astype(BF16), ffn_conv_w[i], ffn_conv_b[i].reshape(1, -1), ffn_w_down[i].astype(BF16))
        h = _ffn_call(h, lat[3], lat[4], lat[5], *ffn)
        if keep_ctx:
            s = _ffn_call(s, cmod[3], cmod[4], cmod[5], *ffn)
    return _final_norm(h, final_g, col)
```

```python
import functools
import math

import jax
import jax.numpy as jnp
from jax import lax
from jax.experimental import pallas as pl
from jax.experimental.pallas import tpu as pltpu

F32 = jnp.float32
BF16 = jnp.bfloat16

GRID_W = 64
RG_C = 8.0
HY_FAST_DECAY = 0.3
HY_SLOW_DECAY = 1.5
HY_TARGET = 1e-2
EPS = 1e-6
TINY = 1e-30
LOG2E = 1.4426950408889634

SUB = 8
LANE = 128
HALO = SUB
ROW_TILE = 512
RG_TT = 64
DFT_FB = 256
COL_CHUNK = 256
FFT_P = 256
FFT_R = 16
FFT_UNROLL = 8
FFT_ROW_UNROLL = 4
VMEM_LIMIT = 56 * 1024 * 1024


def _pick(n, cap, mult):
    best = None
    for d in range(mult, min(n, cap) + 1, mult):
        if n % d == 0:
            best = d
    return n if best is None else best


def _full(arr, single=True):
    nd = arr.ndim
    if single:
        return pl.BlockSpec(arr.shape, lambda *_: (0,) * nd, pipeline_mode=pl.Buffered(1))
    return pl.BlockSpec(arr.shape, lambda *_: (0,) * nd)


def _params(sem):
    return pltpu.CompilerParams(dimension_semantics=sem, vmem_limit_bytes=VMEM_LIMIT)


def _sigmoid(x):
    return 0.5 + 0.5 * jnp.tanh(0.5 * x)


def _gelu_tanh(x):
    return 0.5 * x * (1.0 + jnp.tanh(math.sqrt(2.0 / math.pi) * (x + 0.044715 * (x * x * x))))


def _norm_mod(x, ng, sc, sh):
    ms = jnp.mean(x * x, axis=-1, keepdims=True)
    y = x * lax.rsqrt(ms + EPS)
    return (y * ng) * (1.0 + sc) + sh


def _shift_rows(z, k):
    if k == 0:
        return z
    return pltpu.roll(z, (-k) % z.shape[0], 0)


def _conv_rows(z, cw, cb, offsets, t):
    acc = None
    for i, o in enumerate(offsets):
        term = cw[i:i + 1, :] * _shift_rows(z, o)[HALO:HALO + t]
        acc = term if acc is None else acc + term
    return acc + cb


def _halo_specs(t_rows, l, d):
    r = t_rows // HALO
    last = l // HALO - 1

    def prev_map(b, t):
        return (b, jnp.maximum(t * r - 1, 0), 0)

    def next_map(b, t):
        return (b, jnp.minimum((t + 1) * r, last), 0)

    return (pl.BlockSpec((1, HALO, d), prev_map),
            pl.BlockSpec((1, t_rows, d), lambda b, t: (b, t, 0)),
            pl.BlockSpec((1, HALO, d), next_map))


def _ext_rows(hp_ref, h_ref, hn_ref, ng, sc, sh):
    t = pl.program_id(1)
    nt = pl.num_programs(1)
    x = h_ref[0]
    xm = _norm_mod(x, ng, sc, sh)
    xp = jnp.where(t > 0, _norm_mod(hp_ref[0], ng, sc, sh), 0.0)
    xn = jnp.where(t < nt - 1, _norm_mod(hn_ref[0], ng, sc, sh), 0.0)
    return x, jnp.concatenate([xp, xm, xn], axis=0).astype(BF16)


def _mod_kernel(c_ref, w_ref, b_ref, o_ref):
    cv = c_ref[...]
    ca = cv * _sigmoid(cv)
    o_ref[0] = jnp.dot(ca.astype(BF16), w_ref[0].astype(BF16), preferred_element_type=F32) + b_ref[0]


def _mod_call(cpad, mod_w, mod_b):
    depth, d, m6 = mod_w.shape
    bp = cpad.shape[0]
    tn = _pick(m6, 1536, LANE)
    return pl.pallas_call(
        _mod_kernel,
        grid=(depth, m6 // tn),
        in_specs=[pl.BlockSpec((bp, d), lambda i, j: (0, 0)),
                  pl.BlockSpec((1, d, tn), lambda i, j: (i, 0, j)),
                  pl.BlockSpec((1, 1, tn), lambda i, j: (i, 0, j))],
        out_specs=pl.BlockSpec((1, bp, tn), lambda i, j: (i, 0, j)),
        out_shape=jax.ShapeDtypeStruct((depth, bp, m6), F32),
        compiler_params=_params(("arbitrary", "arbitrary")),
        name="modulation",
    )(cpad, mod_w, mod_b.reshape(depth, 1, m6))


def _store_col_major(o_ref, res, d):
    nr = res.shape[0] // GRID_W
    for rg in range(nr // SUB):
        x4 = res[rg * SUB * GRID_W:(rg + 1) * SUB * GRID_W].reshape(SUB, GRID_W // SUB, SUB, d)
        sw = _swap_slab_sublane([x4[k] for k in range(SUB)])
        o_ref[0, :, rg * SUB:(rg + 1) * SUB, :] = jnp.stack(sw, axis=1).reshape(GRID_W, SUB, d)


def _store_raster(o_ref, y, rows, d):
    nw = y.shape[0] // rows
    for wg in range(nw // SUB):
        y4 = y[wg * SUB * rows:(wg + 1) * SUB * rows].reshape(SUB, rows // SUB, SUB, d)
        sw = _swap_slab_sublane([y4[k] for k in range(SUB)])
        o_ref[0, :, wg * SUB:(wg + 1) * SUB, :] = jnp.stack(sw, axis=1).reshape(rows, SUB, d)


def _rms(xw, g):
    ms = jnp.mean(xw * xw, axis=-1, keepdims=True)
    return xw * lax.rsqrt(ms + EPS) * g


def _ffn_kernel(hp_ref, h_ref, hn_ref, sh_ref, sc_ref, g_ref, ng_ref, wup_ref, cw_ref, cb_ref, wdn_ref, fg_ref,
                o_ref, act_s, *, t_rows, f, ck, out_mode, rows):
    x, xe = _ext_rows(hp_ref, h_ref, hn_ref, ng_ref[...], sc_ref[0], sh_ref[0])
    offs = (-1, 0, 1)
    for ci in range(f // ck):
        lo = ci * ck
        zg = jnp.dot(xe, wup_ref[:, lo:lo + ck], preferred_element_type=F32)
        zu = jnp.dot(xe, wup_ref[:, f + lo:f + lo + ck], preferred_element_type=F32)
        g = _conv_rows(zg, cw_ref[:, lo:lo + ck], cb_ref[:, lo:lo + ck], offs, t_rows)
        u = _conv_rows(zu, cw_ref[:, f + lo:f + lo + ck], cb_ref[:, f + lo:f + lo + ck], offs, t_rows)
        act_s[:, lo:lo + ck] = (g * _sigmoid(g) * u).astype(BF16)
    out = jnp.dot(act_s[...], wdn_ref[...], preferred_element_type=F32)
    res = x + g_ref[0] * out
    d = res.shape[1]
    if out_mode == "col_major":
        _store_col_major(o_ref, res, d)
    elif out_mode == "final_raster":
        _store_raster(o_ref, _rms(res, fg_ref[...]), rows, d)
    else:
        o_ref[0] = res


def _ffn_fusable(l):
    t_rows = min(ROW_TILE, l)
    rows = l // GRID_W
    return t_rows % (SUB * GRID_W) == 0, rows % SUB == 0 and t_rows % (SUB * rows) == 0


def _ffn_call(h, sh, sc, g, ng, wup, cw, cb, wdn, fg, out_mode="plain"):
    b, l, d = h.shape
    f = wdn.shape[0]
    t_rows = min(ROW_TILE, l)
    rows = l // GRID_W
    ck = _pick(f, COL_CHUNK, LANE)
    vec = pl.BlockSpec((1, 1, d), lambda bi, ti: (bi, 0, 0))
    if out_mode == "col_major":
        out_shape, out_block = (b, GRID_W, rows, d), (1, GRID_W, t_rows // GRID_W, d)
    elif out_mode == "final_raster":
        out_shape, out_block = (b, rows, GRID_W, d), (1, rows, t_rows // rows, d)
    else:
        out_shape, out_block = (b, l, d), (1, t_rows, d)
    out_map = (lambda bi, ti: (bi, ti, 0)) if out_mode == "plain" else (lambda bi, ti: (bi, 0, ti, 0))
    kern = functools.partial(_ffn_kernel, t_rows=t_rows, f=f, ck=ck, out_mode=out_mode, rows=rows)
    return pl.pallas_call(
        kern,
        grid=(b, l // t_rows),
        in_specs=[*_halo_specs(t_rows, l, d), vec, vec, vec, _full(ng), _full(wup), _full(cw), _full(cb), _full(wdn),
                  _full(fg)],
        out_specs=pl.BlockSpec(out_block, out_map),
        out_shape=jax.ShapeDtypeStruct(out_shape, F32),
        scratch_shapes=[pltpu.VMEM((t_rows, f), BF16)],
        compiler_params=_params(("parallel", "arbitrary")),
        name="conv_ffn",
    )(h, h, h, sh, sc, g, ng, wup, cw, cb, wdn, fg).reshape(b, l, d)


def _rg_gates(u, k, wa_ref, ba_ref, wi_ref, bi_ref, lam_ref):
    hd = u.shape[1]
    cs = slice(k * hd, (k + 1) * hd)
    ub = u.astype(BF16)
    ta = jnp.tanh(jnp.dot(ub, wa_ref[k], preferred_element_type=F32) + ba_ref[:, cs])
    ti = jnp.tanh(jnp.dot(ub, wi_ref[k], preferred_element_type=F32) + bi_ref[:, cs])
    lam = lam_ref[:, cs]
    softplus_neg = jnp.maximum(-lam, 0.0) + jnp.log1p(jnp.exp(-jnp.abs(lam)))
    half = (0.5 * RG_C) * softplus_neg
    neg_log_a = half + half * ta
    half2 = (-LOG2E) * half
    a = jnp.exp2(half2 + half2 * ta)
    om = jnp.tanh(neg_log_a) * (1.0 + a * a)
    beta = om * lax.rsqrt(jnp.maximum(om, TINY))
    return a, (beta * (0.5 * u)) * (1.0 + ti)


def _swap_slab_sublane(slabs):
    sub = lax.broadcasted_iota(jnp.int32, slabs[0].shape, 1)
    s = SUB // 2
    while s:
        bit = (sub & s) != 0
        new = list(slabs)
        for k in range(SUB):
            if k & s:
                continue
            lo, hi = slabs[k], slabs[k | s]
            new[k] = jnp.where(bit, pltpu.roll(hi, s, 1), lo)
            new[k | s] = jnp.where(bit, hi, pltpu.roll(lo, SUB - s, 1))
        slabs = new
        s //= 2
    return slabs


def _to_time_major(x):
    _, t, c = x.shape
    slabs = _swap_slab_sublane([x[k].reshape(t // SUB, SUB, c) for k in range(SUB)])
    return jnp.stack(slabs, axis=1).reshape(t * SUB, c)


def _to_batch_major(y, t):
    c = y.shape[1]
    y4 = y.reshape(t // SUB, SUB, SUB, c)
    slabs = _swap_slab_sublane([y4[:, k] for k in range(SUB)])
    return jnp.stack([s.reshape(t, c) for s in slabs], axis=0)


def _scan_tile(a_s, b_s, state_s, tt, reverse):
    def step(s, h):
        pos = (tt - 1 - s) if reverse else s
        rows = pl.ds(pl.multiple_of(pos * SUB, SUB), SUB)
        h = a_s[rows, :] * h + b_s[rows, :]
        b_s[rows, :] = h
        return h

    h = lax.fori_loop(0, tt, step, state_s[...], unroll=8)
    state_s[...] = h
    return h


def _rg_fwd_kernel(hp_ref, h_ref, hn_ref, sh_ref, sc_ref, ng_ref, win_ref, cw_ref, cb_ref,
                   wa_ref, ba_ref, wi_ref, bi_ref, lam_ref, h0_ref,
                   gate_ref, u_ref, hf_ref, st_ref, a_s, b_s, state_s, *, tt, d, heads):
    t = pl.program_id(1)
    nt = pl.num_programs(1)
    n = tt * SUB

    @pl.when(t == 0)
    def _():
        state_s[...] = h0_ref[...]

    ng = ng_ref[...]
    sc = sc_ref[0]
    sh = sh_ref[0]

    def nm(rows):
        return _norm_mod(rows.reshape(-1, SUB, d), ng, sc, sh).reshape(rows.shape)

    xm = nm(_to_time_major(h_ref[...]))
    xp = jnp.where(t > 0, nm(_to_time_major(hp_ref[...])[(HALO - 1) * SUB:, :]), 0.0)
    xn = jnp.where(t < nt - 1, nm(_to_time_major(hn_ref[...])[:3 * SUB, :]), 0.0)
    xe = jnp.concatenate([xp, xm, xn], axis=0).astype(BF16)

    xmb = xm.astype(BF16)
    cw = cw_ref[...]
    cb = cb_ref[...]
    hd = d // heads
    for k in range(heads):
        cs = slice(k * hd, (k + 1) * hd)
        gate_ref[0, :, cs] = jnp.dot(xmb, win_ref[:, cs], preferred_element_type=F32).astype(gate_ref.dtype)
        ur = jnp.dot(xe, win_ref[:, d + k * hd:d + (k + 1) * hd], preferred_element_type=F32)
        u = cb[:, cs]
        for i in range(cw.shape[0]):
            u = u + cw[i:i + 1, cs] * ur[i * SUB:i * SUB + n]
        u_ref[0, :, cs] = u
        a, b_in = _rg_gates(u, k, wa_ref, ba_ref, wi_ref, bi_ref, lam_ref)
        a_s[:, cs] = a
        b_s[:, cs] = b_in
    st_ref[...] = _scan_tile(a_s, b_s, state_s, tt, reverse=False)
    hf_ref[0] = b_s[...]


def _rg_bwd_kernel(u_ref, gate_ref, hf_ref, h_ref, g1_ref, wa_ref, ba_ref, wi_ref, bi_ref, lam_ref, h0_ref,
                   wout_ref, o_ref, st_ref, a_s, b_s, state_s, *, tt, d, heads):
    t = pl.program_id(1)

    @pl.when(t == 0)
    def _():
        state_s[...] = h0_ref[...]

    hd = d // heads
    for k in range(heads):
        cs = slice(k * hd, (k + 1) * hd)
        a, b_in = _rg_gates(u_ref[0, :, cs], k, wa_ref, ba_ref, wi_ref, bi_ref, lam_ref)
        a_s[:, cs] = a
        b_s[:, cs] = b_in
    st_ref[...] = _scan_tile(a_s, b_s, state_s, tt, reverse=True)

    y = jnp.concatenate(
        [((hf_ref[0, :, cs] + b_s[:, cs]) * _gelu_tanh(gate_ref[0, :, cs].astype(F32))).astype(BF16)
         for cs in (slice(k * hd, (k + 1) * hd) for k in range(heads))], axis=1)
    out = jnp.dot(y, wout_ref[...], preferred_element_type=F32)
    out = (out.reshape(tt, SUB, d) * g1_ref[0]).reshape(tt * SUB, d)
    o_ref[...] = h_ref[...] + _to_batch_major(out, tt)


def _rg_layer(h, sh, sc, g1, ng, win, cw, cb, wa, ba, wi, bi, lam, wout, h0_f, h0_b):
    b, l, d = h.shape
    heads = wa.shape[1]
    assert b % SUB == 0 and l % HALO == 0, "the recurrent kernels scan SUB batch rows per sublane tile"
    groups = b // SUB
    tt = min(RG_TT, l)
    nt = l // tt
    r = tt // HALO
    n = tt * SUB
    grid = (groups, nt)
    sem = _params(("parallel", "arbitrary"))
    sh, sc, g1 = (v.reshape(groups, SUB, d) for v in (sh, sc, g1))

    vec = pl.BlockSpec((1, SUB, d), lambda gi, ti: (gi, 0, 0))
    st_spec = pl.BlockSpec((SUB, d), lambda gi, ti: (gi, 0))
    tile_f = pl.BlockSpec((SUB, tt, d), lambda gi, ti: (gi, ti, 0))
    tm_f = pl.BlockSpec((1, n, d), lambda gi, ti: (gi, ti, 0))
    prev_spec = pl.BlockSpec((SUB, HALO, d), lambda gi, ti: (gi, jnp.maximum(ti * r - 1, 0), 0))
    next_spec = pl.BlockSpec((SUB, HALO, d), lambda gi, ti: (gi, jnp.minimum((ti + 1) * r, nt * r - 1), 0))
    scratch = [pltpu.VMEM((n, d), F32), pltpu.VMEM((n, d), F32), pltpu.VMEM((SUB, d), F32)]
    tm_shape = (groups, l * SUB, d)

    def dirw(k):
        return wa[k], ba[k], wi[k], bi[k], lam[k]

    fw = dirw(0)
    gate, u, hf, st_f = pl.pallas_call(
        functools.partial(_rg_fwd_kernel, tt=tt, d=d, heads=heads),
        grid=grid,
        in_specs=[prev_spec, tile_f, next_spec, vec, vec, _full(ng), _full(win), _full(cw), _full(cb),
                  *[_full(w) for w in fw], st_spec],
        out_specs=[tm_f, tm_f, tm_f, st_spec],
        out_shape=[jax.ShapeDtypeStruct(tm_shape, BF16), jax.ShapeDtypeStruct(tm_shape, F32),
                   jax.ShapeDtypeStruct(tm_shape, F32), jax.ShapeDtypeStruct((b, d), F32)],
        scratch_shapes=scratch,
        compiler_params=sem,
        name="rg_fwd",
    )(h, h, h, sh, sc, ng, win, cw, cb, *fw, h0_f)

    tile_r = pl.BlockSpec((SUB, tt, d), lambda gi, ti: (gi, nt - 1 - ti, 0))
    tm_r = pl.BlockSpec((1, n, d), lambda gi, ti: (gi, nt - 1 - ti, 0))
    bw = dirw(1)
    h_new, st_b = pl.pallas_call(
        functools.partial(_rg_bwd_kernel, tt=tt, d=d, heads=heads),
        grid=grid,
        in_specs=[tm_r, tm_r, tm_r, tile_r, vec, *[_full(w) for w in bw], st_spec, _full(wout)],
        out_specs=[tile_r, st_spec],
        out_shape=[jax.ShapeDtypeStruct((b, l, d), F32), jax.ShapeDtypeStruct((b, d), F32)],
        scratch_shapes=scratch,
        compiler_params=sem,
        name="rg_bwd",
    )(u, gate, hf, h, g1, *bw, h0_b, wout)
    return h_new, st_f, st_b


def _hy_in_kernel(hp_ref, h_ref, hn_ref, sh_ref, sc_ref, ng_ref, win_ref, cw_ref, cb_ref,
                  x0_ref, xv_ref, *, t_rows, d, ck):
    _, xe = _ext_rows(hp_ref, h_ref, hn_ref, ng_ref[...], sc_ref[0], sh_ref[0])
    offs = (-1, 0, 1)

    def proj(lo):
        z = jnp.dot(xe, win_ref[:, lo:lo + ck], preferred_element_type=F32)
        return _conv_rows(z, cw_ref[:, lo:lo + ck], cb_ref[:, lo:lo + ck], offs, t_rows)

    for ci in range(d // ck):
        lo = ci * ck
        x0_ref[0, :, lo:lo + ck] = proj(lo).astype(BF16)
        xv_ref[0, :, lo:lo + ck] = (proj(d + lo) * proj(2 * d + lo)).astype(BF16)


def _hy_in_call(h, sh, sc, ng, win, cw, cb):
    b, l, d = h.shape
    t_rows = min(ROW_TILE, l)
    ck = _pick(d, COL_CHUNK, LANE)
    vec = pl.BlockSpec((1, 1, d), lambda bi, ti: (bi, 0, 0))
    tile = pl.BlockSpec((1, t_rows, d), lambda bi, ti: (bi, ti, 0))
    return pl.pallas_call(
        functools.partial(_hy_in_kernel, t_rows=t_rows, d=d, ck=ck),
        grid=(b, l // t_rows),
        in_specs=[*_halo_specs(t_rows, l, d), vec, vec, _full(ng), _full(win), _full(cw), _full(cb)],
        out_specs=[tile, tile],
        out_shape=[jax.ShapeDtypeStruct((b, l, d), BF16), jax.ShapeDtypeStruct((b, l, d), BF16)],
        compiler_params=_params(("parallel", "arbitrary")),
        name="hy_in",
    )(h, h, h, sh, sc, ng, win, cw, cb)


def _hy_filter_kernel(w1_ref, b1_ref, w2_ref, b2_ref, w3_ref, b3_ref, w4_ref, fr_ref, k_ref, hdn_s,
                      *, n, d, dc, bands):
    j = pl.program_id(0)
    hp = lax.Precision.HIGHEST

    @pl.when(j == 0)
    def _():
        s = lax.broadcasted_iota(jnp.int32, (n, LANE), 0).astype(F32)
        lane = lax.broadcasted_iota(jnp.int32, (n, LANE), 1)
        tpos = s / float(n - 1)
        w = s * (2.0 * math.pi / n)
        bidx = jnp.where(lane <= bands, lane - 1, lane - 1 - bands).astype(F32)
        band = 1e-4 + bidx * ((bands - 1 - 1e-4) / (bands - 1))
        arg = band * w
        z = jnp.where(lane == 0, tpos,
                      jnp.where(lane <= bands, jnp.cos(arg),
                                jnp.where(lane <= 2 * bands, -jnp.sin(arg), 0.0)))
        fr = fr_ref[...]
        hd = jnp.sin(fr * (jnp.dot(z, w1_ref[...], precision=hp, preferred_element_type=F32) + b1_ref[...]))
        hd = jnp.sin(fr * (jnp.dot(hd, w2_ref[...], precision=hp, preferred_element_type=F32) + b2_ref[...]))
        hd = jnp.sin(fr * (jnp.dot(hd, w3_ref[...], precision=hp, preferred_element_type=F32) + b3_ref[...]))
        hdn_s[...] = hd

    k = jnp.dot(hdn_s[...], w4_ref[...], precision=hp, preferred_element_type=F32)
    srow = lax.broadcasted_iota(jnp.int32, (n, 1), 0).astype(F32)
    centre = n // 2
    dist = jnp.abs(srow - float(centre)) / float(centre)
    ch = (j * dc + lax.broadcasted_iota(jnp.int32, (1, dc), 1)).astype(F32)
    d_lo = math.log(HY_TARGET) / HY_SLOW_DECAY
    d_hi = math.log(HY_TARGET) / HY_FAST_DECAY
    delta = d_lo + ch * ((d_hi - d_lo) / (d - 1))
    k = k * jnp.exp(-dist * jnp.abs(delta))
    k = k / jnp.sum(jnp.abs(k), axis=0, keepdims=True)
    k_ref[...] = k.astype(k_ref.dtype)


def _hy_filter_call(n, w1, b1, w2, b2, w3, b3, w4, fr):
    emb, fd = w1.shape
    d = w4.shape[1]
    bands = (emb - 1) // 2
    w1p = jnp.zeros((LANE, fd), F32).at[:emb].set(w1)
    dc = _pick(d, COL_CHUNK, LANE)
    args = (w1p, b1.reshape(1, fd), w2, b2.reshape(1, fd), w3, b3.reshape(1, fd))
    return pl.pallas_call(
        functools.partial(_hy_filter_kernel, n=n, d=d, dc=dc, bands=bands),
        grid=(d // dc,),
        in_specs=[*[_full(a, single=False) for a in args],
                  pl.BlockSpec((fd, dc), lambda j: (0, j)), _full(fr.reshape(1, fd), single=False)],
        out_specs=pl.BlockSpec((n, dc), lambda j: (0, j)),
        out_shape=jax.ShapeDtypeStruct((n, d), BF16),
        scratch_shapes=[pltpu.VMEM((n, fd), F32)],
        compiler_params=_params(("arbitrary",)),
        name="hy_filter",
    )(*args, w4, fr.reshape(1, fd))


def _dft_mats(n):
    i = 2 * jnp.arange(n, dtype=jnp.int32) + 1
    m = (i[:, None] * i[None, :]) % (8 * n)
    ang = m.astype(F32) * (2.0 * math.pi / (8 * n))
    return jnp.cos(ang).astype(BF16), jnp.sin(ang).astype(BF16)


def _hy_spec_kernel(c_ref, s_ref, k_ref, kr_ref, ki_ref, *, n, fb):
    j = pl.program_id(0)
    kk = k_ref[...]
    a = jnp.dot(c_ref[...], kk, preferred_element_type=F32)
    b = jnp.dot(s_ref[...], kk, preferred_element_type=F32)
    fi = 2 * (j * fb + lax.broadcasted_iota(jnp.int32, (fb, 1), 0)) + 1
    m = lax.rem(fi * (n + 1), 8 * n)
    ang = m.astype(F32) * (2.0 * math.pi / (8 * n))
    qr = jnp.cos(ang) * (1.0 / n)
    qi = jnp.sin(ang) * (1.0 / n)
    kr_ref[...] = qr * a + qi * b
    ki_ref[...] = qi * a - qr * b


def _hy_spec_call(cm, sm, k):
    n, d = k.shape
    fb = min(DFT_FB, n)
    row = pl.BlockSpec((fb, n), lambda j: (j, 0))
    out = pl.BlockSpec((fb, d), lambda j: (j, 0))
    return pl.pallas_call(
        functools.partial(_hy_spec_kernel, n=n, fb=fb),
        grid=(n // fb,),
        in_specs=[row, row, _full(k)],
        out_specs=[out, out],
        out_shape=[jax.ShapeDtypeStruct((n, d), F32), jax.ShapeDtypeStruct((n, d), F32)],
        compiler_params=_params(("arbitrary",)),
        name="hy_spec_dense",
    )(cm, sm, k)


def _hy_conv_kernel(c_ref, s_ref, xv_ref, kr_ref, ki_ref, x0_ref, h_ref, skip_ref, g1_ref, wout_ref,
                    o_ref, zr_s, zi_s, *, fb):
    p = pl.program_id(1)
    j = pl.program_id(2)
    rows = pl.ds(pl.multiple_of(j * fb, fb), fb)

    @pl.when(p == 0)
    def _():
        xv = xv_ref[0]
        xr = jnp.dot(c_ref[...], xv, preferred_element_type=F32)
        xi = -jnp.dot(s_ref[...], xv, preferred_element_type=F32)
        kr = kr_ref[...]
        ki = ki_ref[...]
        zr_s[rows, :] = (xr * kr - xi * ki).astype(BF16)
        zi_s[rows, :] = (xr * ki + xi * kr).astype(BF16)

    @pl.when(p == 1)
    def _():
        y = (jnp.dot(c_ref[...], zr_s[...], preferred_element_type=F32)
             - jnp.dot(s_ref[...], zi_s[...], preferred_element_type=F32))
        y = y + xv_ref[0, rows, :].astype(F32) * skip_ref[...]
        pv = (x0_ref[0].astype(F32) * y).astype(BF16)
        out = jnp.dot(pv, wout_ref[...], preferred_element_type=F32)
        o_ref[0] = h_ref[0] + g1_ref[0] * out


def _hy_conv_call(cm, sm, xv, kr, ki, x0, h, skip, g1, wout):
    b, n, d = h.shape
    fb = min(DFT_FB, n)
    nf = n // fb
    row = pl.BlockSpec((fb, n), lambda bi, p, j: (j, 0))
    kspec = pl.BlockSpec((fb, d), lambda bi, p, j: (jnp.where(p == 0, j, nf - 1), 0))
    tile = pl.BlockSpec((1, fb, d), lambda bi, p, j: (bi, jnp.where(p == 1, j, 0), 0))
    return pl.pallas_call(
        functools.partial(_hy_conv_kernel, fb=fb),
        grid=(b, 2, nf),
        in_specs=[row, row,
                  pl.BlockSpec((1, n, d), lambda bi, p, j: (bi, 0, 0), pipeline_mode=pl.Buffered(1)),
                  kspec, kspec, tile, tile, _full(skip),
                  pl.BlockSpec((1, 1, d), lambda bi, p, j: (bi, 0, 0)), _full(wout)],
        out_specs=tile,
        out_shape=jax.ShapeDtypeStruct((b, n, d), F32),
        scratch_shapes=[pltpu.VMEM((n, d), BF16), pltpu.VMEM((n, d), BF16)],
        compiler_params=_params(("parallel", "arbitrary", "arbitrary")),
        name="hy_conv_dense",
    )(cm, sm, xv, kr, ki, x0, h, skip, g1, wout)


def _fft_mats(n):
    p, r = FFT_P, FFT_R
    nn = 2 * n
    q, qh, hh = nn // p, n // p, p // r
    i32 = jnp.int32
    t1h = jnp.arange(hh, dtype=i32).reshape(hh, 1, 1, 1)
    f2 = jnp.arange(q, dtype=i32).reshape(1, q, 1, 1)
    t1l = jnp.arange(r, dtype=i32).reshape(1, 1, r, 1)
    t2 = jnp.arange(qh, dtype=i32).reshape(1, 1, 1, qh)
    m = ((2 * f2 + 1) * (p * t2 + r * t1h + t1l)) % (2 * nn)
    th = m.astype(F32) * (math.pi / nn)
    base = jnp.stack([jnp.cos(th), -jnp.sin(th)], axis=1).astype(BF16)
    eye = jnp.eye(r, dtype=BF16)
    w1 = base[..., None] * eye[None, None, None, :, None, :]
    w1f = w1.reshape(hh, 2 * q * r, qh * r)
    w1i = jnp.transpose(w1, (0, 4, 5, 1, 2, 3)).reshape(hh, qh * r, 2 * q * r)
    f1 = jnp.arange(p // 2, dtype=i32)[:, None]
    t1 = jnp.arange(p, dtype=i32)[None, :]
    phi = ((f1 * t1) % p).astype(F32) * (2.0 * math.pi / p)
    c, sn = jnp.cos(phi), jnp.sin(phi)
    w2 = jnp.block([[c, sn], [-sn, c]]).astype(BF16)
    w2i = jnp.block([[c.T, -sn.T], [sn.T, c.T]]).astype(BF16)
    return w1f, w1i, w2, w2i


def _fft_coarse_fwd(load_rows, w1_ref, s_ref, n):
    p, r = FFT_P, FFT_R
    q, qh, hh = 2 * n // p, n // p, p // r

    def body(t1h, carry):
        u = jnp.concatenate([load_rows(pl.multiple_of(p * t2 + t1h * r, r)) for t2 in range(qh)], axis=0)
        res = jnp.dot(w1_ref[t1h], u, preferred_element_type=F32)
        for pf in range(2 * q):
            s_ref[pf * hh + t1h] = res[pf * r:(pf + 1) * r].astype(BF16)
        return carry

    lax.fori_loop(0, hh, body, 0, unroll=FFT_ROW_UNROLL)


def _fft_fine_fwd(s_ref, w2_ref, f2, n):
    p, r = FFT_P, FFT_R
    q, hh = 2 * n // p, p // r
    cc = s_ref.shape[-1]
    v = jnp.concatenate([s_ref[pl.ds(f2 * hh, hh)].reshape(p, cc),
                         s_ref[pl.ds((q + f2) * hh, hh)].reshape(p, cc)], axis=0)
    return jnp.dot(w2_ref[...], v, preferred_element_type=F32)


def _hy_kspec2_kernel(w1_ref, w2_ref, k_ref, o_ref, s_ref, *, n):
    p = FFT_P
    q = 2 * n // p
    _fft_coarse_fwd(lambda r0: k_ref[pl.ds(r0, FFT_R), :], w1_ref, s_ref, n)

    def body(f2, carry):
        xs = _fft_fine_fwd(s_ref, w2_ref, f2, n)
        xr, xi = xs[:p // 2], xs[p // 2:]
        f1 = lax.broadcasted_iota(jnp.int32, (p // 2, 1), 0)
        m = lax.rem(2 * (q * f1 + f2) + 1, 8)
        ang = m.astype(F32) * (math.pi / 4.0)
        qr = jnp.cos(ang) * (1.0 / n)
        qi = jnp.sin(ang) * (1.0 / n)
        o_ref[f2, 0] = qr * xr - qi * xi
        o_ref[f2, 1] = qr * xi + qi * xr
        return carry

    lax.fori_loop(0, q, body, 0)


def _hy_kspec2_call(w1, w2, k):
    n, d = k.shape
    p, r = FFT_P, FFT_R
    q, hh = 2 * n // p, p // r
    cc = _pick(d, COL_CHUNK, LANE)
    return pl.pallas_call(
        functools.partial(_hy_kspec2_kernel, n=n),
        grid=(d // cc,),
        in_specs=[_full(w1), _full(w2), pl.BlockSpec((n, cc), lambda c: (0, c))],
        out_specs=pl.BlockSpec((q, 2, p // 2, cc), lambda c: (0, 0, 0, c)),
        out_shape=jax.ShapeDtypeStruct((q, 2, p // 2, d), F32),
        scratch_shapes=[pltpu.VMEM((2 * q * hh, r, cc), BF16)],
        compiler_params=_params(("arbitrary",)),
        name="hy_kspec",
    )(w1, w2, k)


def _hy_conv2_kernel(w1_ref, w1i_ref, w2_ref, w2i_ref, xv_ref, x0_ref, k_ref, skip_ref, p_ref, s_ref, *, n):
    p, r = FFT_P, FFT_R
    q, qh, hh = 2 * n // p, n // p, p // r
    _fft_coarse_fwd(lambda r0: xv_ref[0, pl.ds(r0, r), :], w1_ref, s_ref, n)

    def mid(f2, carry):
        xs = _fft_fine_fwd(s_ref, w2_ref, f2, n)
        xr, xi = xs[:p // 2], xs[p // 2:]
        kr = k_ref[f2, 0]
        ki = k_ref[f2, 1]
        z = jnp.concatenate([xr * kr - xi * ki, xr * ki + xi * kr], axis=0).astype(BF16)
        c = jnp.dot(w2i_ref[...], z, preferred_element_type=F32)
        cc = c.shape[-1]
        s_ref[pl.ds(f2 * hh, hh)] = c[:p].reshape(hh, r, cc).astype(BF16)
        s_ref[pl.ds((q + f2) * hh, hh)] = c[p:].reshape(hh, r, cc).astype(BF16)
        return carry

    lax.fori_loop(0, q, mid, 0, unroll=FFT_UNROLL)
    skip = skip_ref[...]

    def back(t1h, carry):
        g = jnp.concatenate([s_ref[pf * hh + t1h] for pf in range(2 * q)], axis=0)
        y = jnp.dot(w1i_ref[t1h], g, preferred_element_type=F32)
        for t2 in range(qh):
            rows = pl.ds(pl.multiple_of(p * t2 + t1h * r, r), r)
            yv = y[t2 * r:(t2 + 1) * r] + xv_ref[0, rows, :].astype(F32) * skip
            p_ref[0, rows, :] = (x0_ref[0, rows, :].astype(F32) * yv).astype(p_ref.dtype)
        return carry

    lax.fori_loop(0, hh, back, 0, unroll=FFT_ROW_UNROLL)


def _hy_conv2_call(mats, xv, x0, kspec, skip):
    b, n, d = xv.shape
    p, r = FFT_P, FFT_R
    q, hh = 2 * n // p, p // r
    cc = _pick(d, COL_CHUNK, LANE)
    w1, w1i, w2, w2i = mats
    col = pl.BlockSpec((1, n, cc), lambda c, bi: (bi, 0, c))
    return pl.pallas_call(
        functools.partial(_hy_conv2_kernel, n=n),
        grid=(d // cc, b),
        in_specs=[_full(w1), _full(w1i), _full(w2), _full(w2i), col, col,
                  pl.BlockSpec((q, 2, p // 2, cc), lambda c, bi: (0, 0, 0, c), pipeline_mode=pl.Buffered(1)),
                  pl.BlockSpec((1, cc), lambda c, bi: (0, c))],
        out_specs=col,
        out_shape=jax.ShapeDtypeStruct((b, n, d), BF16),
        scratch_shapes=[pltpu.VMEM((2 * q * hh, r, cc), BF16)],
        compiler_params=_params(("arbitrary", "arbitrary")),
        name="hy_conv2",
    )(w1, w1i, w2, w2i, xv, x0, kspec, skip)


def _hy_out_kernel(p_ref, h_ref, g1_ref, wout_ref, o_ref):
    o_ref[0] = h_ref[0] + g1_ref[0] * jnp.dot(p_ref[0], wout_ref[...], preferred_element_type=F32)


def _hy_out_call(pv, h, g1, wout):
    b, l, d = h.shape
    t_rows = min(2 * ROW_TILE, l)
    tile = pl.BlockSpec((1, t_rows, d), lambda bi, ti: (bi, ti, 0))
    return pl.pallas_call(
        _hy_out_kernel,
        grid=(b, l // t_rows),
        in_specs=[tile, tile, pl.BlockSpec((1, 1, d), lambda bi, ti: (bi, 0, 0)), _full(wout)],
        out_specs=tile,
        out_shape=jax.ShapeDtypeStruct((b, l, d), F32),
        compiler_params=_params(("parallel", "arbitrary")),
        name="hy_out",
    )(pv, h, g1, wout)


def _to_col_kernel(x_ref, o_ref, *, rows, d):
    x4 = x_ref[0].reshape(rows // SUB, SUB, SUB, d)
    res = _swap_slab_sublane([x4[:, k] for k in range(SUB)])
    for w in range(SUB):
        o_ref[0, w * rows:(w + 1) * rows, :] = res[w].reshape(rows, d)


def _to_col_major(x):
    b, l, d = x.shape
    rows = l // GRID_W
    assert rows % SUB == 0
    return pl.pallas_call(
        functools.partial(_to_col_kernel, rows=rows, d=d),
        grid=(b, GRID_W // SUB),
        in_specs=[pl.BlockSpec((1, rows, SUB, d), lambda bi, wi: (bi, 0, wi, 0))],
        out_specs=pl.BlockSpec((1, SUB * rows, d), lambda bi, wi: (bi, wi, 0)),
        out_shape=jax.ShapeDtypeStruct((b, l, d), x.dtype),
        compiler_params=_params(("parallel", "arbitrary")),
        name="to_col_major",
    )(x.reshape(b, rows, GRID_W, d))


def _final_col_kernel(x_ref, g_ref, o_ref, *, rows, d):
    g = g_ref[...]
    slabs = [_rms(x_ref[0, w * rows:(w + 1) * rows, :], g).reshape(rows // SUB, SUB, d) for w in range(SUB)]
    res = _swap_slab_sublane(slabs)
    o_ref[0] = jnp.stack(res, axis=1).reshape(rows, SUB, d)


def _final_row_kernel(x_ref, g_ref, o_ref):
    o_ref[0] = _rms(x_ref[0], g_ref[...])


def _final_norm(x, g, col_major):
    b, l, d = x.shape
    g2 = g.reshape(1, d)
    if not col_major:
        t_rows = min(ROW_TILE, l)
        tile = pl.BlockSpec((1, t_rows, d), lambda bi, ti: (bi, ti, 0))
        return pl.pallas_call(
            _final_row_kernel, grid=(b, l // t_rows), in_specs=[tile, _full(g2)], out_specs=tile,
            out_shape=jax.ShapeDtypeStruct((b, l, d), F32),
            compiler_params=_params(("parallel", "arbitrary")),
            name="final_norm",
        )(x, g2)
    rows = l // GRID_W
    assert rows % SUB == 0
    out = pl.pallas_call(
        functools.partial(_final_col_kernel, rows=rows, d=d),
        grid=(b, GRID_W // SUB),
        in_specs=[pl.BlockSpec((1, SUB * rows, d), lambda bi, wi: (bi, wi, 0)), _full(g2)],
        out_specs=pl.BlockSpec((1, rows, SUB, d), lambda bi, wi: (bi, 0, wi, 0)),
        out_shape=jax.ShapeDtypeStruct((b, rows, GRID_W, d), F32),
        compiler_params=_params(("parallel", "arbitrary")),
        name="final_norm_col",
    )(x, g2)
    return out.reshape(b, l, d)


def kernel(x, c, ctx, c_ctx, mod_w, mod_b, norm1_g, norm2_g, final_g, rg_w_in, rg_conv_w, rg_conv_b, rg_w_a, rg_b_a, rg_w_i, rg_b_i, rg_lam, rg_w_out, hy_w_in, hy_short_w, hy_short_b, hy_pe_w1, hy_pe_b1, hy_pe_w2, hy_pe_b2, hy_pe_w3, hy_pe_b3, hy_pe_w4, hy_freq, hy_skip, hy_w_out, ffn_w_up, ffn_conv_w, ffn_conv_b, ffn_w_down):
    b, l, d = x.shape
    lc = ctx.shape[1]
    depth = mod_w.shape[0]
    n_mixers = 2
    ctx_needed = [any((q % n_mixers) == 0 for q in range(i + 1, depth)) for i in range(depth)]

    bp = -(-(b + 1) // HALO) * HALO
    cpad = jnp.zeros((bp, d), F32).at[:b].set(c).at[b].set(c_ctx)
    mods = _mod_call(cpad, mod_w, mod_b)

    dft = {}
    h = x
    s = ctx
    col = False
    zeros_state = jnp.zeros((b, d), F32)
    fg = final_g.reshape(1, d)
    for i in range(depth):
        kind = i % n_mixers
        j = i // n_mixers
        col_major = j % 2 == 1
        keep_ctx = ctx_needed[i]
        if col_major != col:
            assert col_major, "column-major layers are expected to be contiguous at the end"
            h = _to_col_major(h)
            col = True
        lat = [mods[i, :b, q * d:(q + 1) * d].reshape(b, 1, d) for q in range(6)]
        cmod = [jnp.broadcast_to(mods[i, b, q * d:(q + 1) * d].reshape(1, 1, d), (b, 1, d)) for q in range(6)]
        ng1 = norm1_g[i].reshape(1, d)
        ng2 = norm2_g[i].reshape(1, d)

        if kind == 0:
            w = (ng1, rg_w_in[j].astype(BF16), rg_conv_w[j], rg_conv_b[j].reshape(1, d),
                 (0.5 * rg_w_a[j]).astype(BF16), 0.5 * rg_b_a[j].reshape(2, 1, d), (0.5 * rg_w_i[j]).astype(BF16),
                 0.5 * rg_b_i[j].reshape(2, 1, d), rg_lam[j].reshape(2, 1, d), rg_w_out[j].astype(BF16))
            s_new, st_f, st_b = _rg_layer(s, cmod[0], cmod[1], cmod[2], *w, zeros_state, zeros_state)
            h, _, _ = _rg_layer(h, lat[0], lat[1], lat[2], *w, st_f, st_b)
            if keep_ctx:
                s = s_new
        else:
            win = hy_w_in[j].astype(BF16)
            cw = hy_short_w[j]
            cb = hy_short_b[j].reshape(1, 3 * d)
            wout = hy_w_out[j].astype(BF16)
            skip = hy_skip[j].reshape(1, d)
            pe = (hy_pe_w1[j], hy_pe_b1[j], hy_pe_w2[j], hy_pe_b2[j], hy_pe_w3[j], hy_pe_b3[j], hy_pe_w4[j],
                  hy_freq[j])
            streams = [(h, lat, True)] + ([(s, cmod, False)] if keep_ctx else [])
            for stream, md, is_lat in streams:
                n = stream.shape[1]
                two_level = n % FFT_P == 0 and n // FFT_P >= 2
                if n not in dft:
                    dft[n] = _fft_mats(n) if two_level else _dft_mats(n)
                kf = _hy_filter_call(n, *pe)
                x0, xv = _hy_in_call(stream, md[0], md[1], ng1, win, cw, cb)
                if two_level:
                    w1, w1i, w2, w2i = dft[n]
                    pv = _hy_conv2_call(dft[n], xv, x0, _hy_kspec2_call(w1, w2, kf), skip)
                    new = _hy_out_call(pv, stream, md[2], wout)
                else:
                    cm, sm = dft[n]
                    kr, ki = _hy_spec_call(cm, sm, kf)
                    new = _hy_conv_call(cm, sm, xv, kr, ki, x0, stream, skip, md[2], wout)
                if is_lat:
                    h = new
                else:
                    s = new

        ffn = (ng2, ffn_w_up[i].astype(BF16), ffn_conv_w[i], ffn_conv_b[i].reshape(1, -1), ffn_w_down[i].astype(BF16))
        fuse_col, fuse_final = _ffn_fusable(l)
        next_col = i + 1 < depth and ((i + 1) // n_mixers) % 2 == 1
        mode = "plain"
        if i + 1 == depth and col and fuse_final:
            mode = "final_raster"
        elif next_col and not col and fuse_col:
            mode = "col_major"
        h = _ffn_call(h, lat[3], lat[4], lat[5], *ffn, fg, out_mode=mode)
        if mode == "final_raster":
            return h
        col = col or mode == "col_major"
        if keep_ctx:
            s = _ffn_call(s, cmod[3], cmod[4], cmod[5], *ffn, fg)
    return _final_norm(h, final_g, col)
```

```python
import functools
import math

import jax
import jax.numpy as jnp
from jax import lax
from jax.experimental import pallas as pl
from jax.experimental.pallas import tpu as pltpu

F32 = jnp.float32
BF16 = jnp.bfloat16

GRID_W = 64
RG_C = 8.0
HY_FAST_DECAY = 0.3
HY_SLOW_DECAY = 1.5
HY_TARGET = 1e-2
EPS = 1e-6
TINY = 1e-30
LOG2E = 1.4426950408889634

SUB = 8
LANE = 128
HALO = SUB
ROW_TILE = 512
RG_TT = 128
DFT_FB = 256
COL_CHUNK = 256
FFT_P = 256
FFT_R = 16
FFT_UNROLL = True
VMEM_LIMIT = 56 * 1024 * 1024


def _pick(n, cap, mult):
    best = None
    for d in range(mult, min(n, cap) + 1, mult):
        if n % d == 0:
            best = d
    return n if best is None else best


def _full(arr, single=True):
    nd = arr.ndim
    if single:
        return pl.BlockSpec(arr.shape, lambda *_: (0,) * nd, pipeline_mode=pl.Buffered(1))
    return pl.BlockSpec(arr.shape, lambda *_: (0,) * nd)


def _params(sem):
    return pltpu.CompilerParams(dimension_semantics=sem, vmem_limit_bytes=VMEM_LIMIT)


def _sigmoid(x):
    return 0.5 + 0.5 * jnp.tanh(0.5 * x)


def _gelu_tanh(x):
    c = math.sqrt(2.0 / math.pi)
    return (0.5 * x) * (1.0 + jnp.tanh(x * (c + (c * 0.044715) * (x * x))))


def _norm_mod(x, ng, sc, sh):
    ms = jnp.mean(x * x, axis=-1, keepdims=True)
    y = x * lax.rsqrt(ms + EPS)
    return (y * ng) * (1.0 + sc) + sh


def _shift_rows(z, k):
    if k == 0:
        return z
    return pltpu.roll(z, (-k) % z.shape[0], 0)


def _conv_rows(z, cw, cb, offsets, t):
    acc = None
    for i, o in enumerate(offsets):
        term = cw[i:i + 1, :] * _shift_rows(z, o)[HALO:HALO + t]
        acc = term if acc is None else acc + term
    return acc + cb


def _halo_specs(t_rows, l, d):
    r = t_rows // HALO
    last = l // HALO - 1

    def prev_map(b, t):
        return (b, jnp.maximum(t * r - 1, 0), 0)

    def next_map(b, t):
        return (b, jnp.minimum((t + 1) * r, last), 0)

    return (pl.BlockSpec((1, HALO, d), prev_map),
            pl.BlockSpec((1, t_rows, d), lambda b, t: (b, t, 0)),
            pl.BlockSpec((1, HALO, d), next_map))


def _ext_rows(hp_ref, h_ref, hn_ref, ng, sc, sh):
    t = pl.program_id(1)
    nt = pl.num_programs(1)
    x = h_ref[0]
    xm = _norm_mod(x, ng, sc, sh)
    xp = jnp.where(t > 0, _norm_mod(hp_ref[0], ng, sc, sh), 0.0)
    xn = jnp.where(t < nt - 1, _norm_mod(hn_ref[0], ng, sc, sh), 0.0)
    return x, jnp.concatenate([xp, xm, xn], axis=0).astype(BF16)


def _mod_kernel(c_ref, w_ref, b_ref, o_ref):
    cv = c_ref[...]
    ca = cv * _sigmoid(cv)
    o_ref[0] = jnp.dot(ca.astype(BF16), w_ref[0].astype(BF16), preferred_element_type=F32) + b_ref[0]


def _mod_call(cpad, mod_w, mod_b):
    depth, d, m6 = mod_w.shape
    bp = cpad.shape[0]
    tn = _pick(m6, 1536, LANE)
    return pl.pallas_call(
        _mod_kernel,
        grid=(depth, m6 // tn),
        in_specs=[pl.BlockSpec((bp, d), lambda i, j: (0, 0)),
                  pl.BlockSpec((1, d, tn), lambda i, j: (i, 0, j)),
                  pl.BlockSpec((1, 1, tn), lambda i, j: (i, 0, j))],
        out_specs=pl.BlockSpec((1, bp, tn), lambda i, j: (i, 0, j)),
        out_shape=jax.ShapeDtypeStruct((depth, bp, m6), F32),
        compiler_params=_params(("arbitrary", "arbitrary")),
        name="modulation",
    )(cpad, mod_w, mod_b.reshape(depth, 1, m6))


def _store_col_major(o_ref, res, d):
    nr = res.shape[0] // GRID_W
    for rg in range(nr // SUB):
        x4 = res[rg * SUB * GRID_W:(rg + 1) * SUB * GRID_W].reshape(SUB, GRID_W // SUB, SUB, d)
        sw = _swap_slab_sublane([x4[k] for k in range(SUB)])
        o_ref[0, :, rg * SUB:(rg + 1) * SUB, :] = jnp.stack(sw, axis=1).reshape(GRID_W, SUB, d)


def _store_raster(o_ref, y, rows, d):
    nw = y.shape[0] // rows
    for wg in range(nw // SUB):
        y4 = y[wg * SUB * rows:(wg + 1) * SUB * rows].reshape(SUB, rows // SUB, SUB, d)
        sw = _swap_slab_sublane([y4[k] for k in range(SUB)])
        o_ref[0, :, wg * SUB:(wg + 1) * SUB, :] = jnp.stack(sw, axis=1).reshape(rows, SUB, d)


def _rms(xw, g):
    ms = jnp.mean(xw * xw, axis=-1, keepdims=True)
    return xw * lax.rsqrt(ms + EPS) * g


def _ffn_kernel(hp_ref, h_ref, hn_ref, sh_ref, sc_ref, g_ref, ng_ref, wup_ref, cw_ref, cb_ref, wdn_ref, fg_ref,
                o_ref, act_s, *, t_rows, f, ck, out_mode, rows):
    x, xe = _ext_rows(hp_ref, h_ref, hn_ref, ng_ref[...], sc_ref[0], sh_ref[0])
    offs = (-1, 0, 1)
    for ci in range(f // ck):
        lo = ci * ck
        zg = jnp.dot(xe, wup_ref[:, lo:lo + ck], preferred_element_type=F32)
        zu = jnp.dot(xe, wup_ref[:, f + lo:f + lo + ck], preferred_element_type=F32)
        g = _conv_rows(zg, cw_ref[:, lo:lo + ck], cb_ref[:, lo:lo + ck], offs, t_rows)
        u = _conv_rows(zu, cw_ref[:, f + lo:f + lo + ck], cb_ref[:, f + lo:f + lo + ck], offs, t_rows)
        act_s[:, lo:lo + ck] = (g * _sigmoid(g) * u).astype(BF16)
    out = jnp.dot(act_s[...], wdn_ref[...], preferred_element_type=F32)
    res = x + g_ref[0] * out
    d = res.shape[1]
    if out_mode == "col_major":
        _store_col_major(o_ref, res, d)
    elif out_mode == "final_raster":
        _store_raster(o_ref, _rms(res, fg_ref[...]), rows, d)
    else:
        o_ref[0] = res


def _ffn_fusable(l):
    t_rows = min(ROW_TILE, l)
    rows = l // GRID_W
    return t_rows % (SUB * GRID_W) == 0, rows % SUB == 0 and t_rows % (SUB * rows) == 0


def _ffn_call(h, sh, sc, g, ng, wup, cw, cb, wdn, fg, out_mode="plain"):
    b, l, d = h.shape
    f = wdn.shape[0]
    t_rows = min(ROW_TILE, l)
    rows = l // GRID_W
    ck = _pick(f, COL_CHUNK, LANE)
    vec = pl.BlockSpec((1, 1, d), lambda bi, ti: (bi, 0, 0))
    if out_mode == "col_major":
        out_shape, out_block = (b, GRID_W, rows, d), (1, GRID_W, t_rows // GRID_W, d)
    elif out_mode == "final_raster":
        out_shape, out_block = (b, rows, GRID_W, d), (1, rows, t_rows // rows, d)
    else:
        out_shape, out_block = (b, l, d), (1, t_rows, d)
    out_map = (lambda bi, ti: (bi, ti, 0)) if out_mode == "plain" else (lambda bi, ti: (bi, 0, ti, 0))
    kern = functools.partial(_ffn_kernel, t_rows=t_rows, f=f, ck=ck, out_mode=out_mode, rows=rows)
    return pl.pallas_call(
        kern,
        grid=(b, l // t_rows),
        in_specs=[*_halo_specs(t_rows, l, d), vec, vec, vec, _full(ng), _full(wup), _full(cw), _full(cb), _full(wdn),
                  _full(fg)],
        out_specs=pl.BlockSpec(out_block, out_map),
        out_shape=jax.ShapeDtypeStruct(out_shape, F32),
        scratch_shapes=[pltpu.VMEM((t_rows, f), BF16)],
        compiler_params=_params(("parallel", "arbitrary")),
        name="conv_ffn",
    )(h, h, h, sh, sc, g, ng, wup, cw, cb, wdn, fg).reshape(b, l, d)


def _rg_gates(u, k, wa_ref, ba_ref, wi_ref, bi_ref, lam_ref):
    hd = u.shape[1]
    cs = slice(k * hd, (k + 1) * hd)
    ub = u.astype(BF16)
    ta = jnp.tanh(jnp.dot(ub, wa_ref[k], preferred_element_type=F32) + ba_ref[:, cs])
    ti = jnp.tanh(jnp.dot(ub, wi_ref[k], preferred_element_type=F32) + bi_ref[:, cs])
    lam = lam_ref[:, cs]
    softplus_neg = jnp.maximum(-lam, 0.0) + jnp.log1p(jnp.exp(-jnp.abs(lam)))
    half = (0.5 * RG_C) * softplus_neg
    neg_log_a = half + half * ta
    a = jnp.exp2(neg_log_a * (-LOG2E))
    om = jnp.tanh(neg_log_a) * (1.0 + a * a)
    beta = om * lax.rsqrt(jnp.maximum(om, TINY))
    return a, (beta * (0.5 * u)) * (1.0 + ti)


def _swap_slab_sublane(slabs):
    sub = lax.broadcasted_iota(jnp.int32, slabs[0].shape, 1)
    s = SUB // 2
    while s:
        bit = (sub & s) != 0
        new = list(slabs)
        for k in range(SUB):
            if k & s:
                continue
            lo, hi = slabs[k], slabs[k | s]
            new[k] = jnp.where(bit, pltpu.roll(hi, s, 1), lo)
            new[k | s] = jnp.where(bit, hi, pltpu.roll(lo, SUB - s, 1))
        slabs = new
        s //= 2
    return slabs


def _to_time_major(x):
    _, t, c = x.shape
    slabs = _swap_slab_sublane([x[k].reshape(t // SUB, SUB, c) for k in range(SUB)])
    return jnp.stack(slabs, axis=1).reshape(t * SUB, c)


def _to_batch_major(y, t):
    c = y.shape[1]
    y4 = y.reshape(t // SUB, SUB, SUB, c)
    slabs = _swap_slab_sublane([y4[:, k] for k in range(SUB)])
    return jnp.stack([s.reshape(t, c) for s in slabs], axis=0)


def _scan_tile(a_s, b_s, state_s, tt, reverse):
    def step(s, h):
        pos = (tt - 1 - s) if reverse else s
        rows = pl.ds(pl.multiple_of(pos * SUB, SUB), SUB)
        h = a_s[rows, :] * h + b_s[rows, :]
        b_s[rows, :] = h
        return h

    h = lax.fori_loop(0, tt, step, state_s[...], unroll=8)
    state_s[...] = h
    return h


def _rg_fwd_kernel(hp_ref, h_ref, hn_ref, sh_ref, sc_ref, ng_ref, win_ref, cw_ref, cb_ref,
                   wa_ref, ba_ref, wi_ref, bi_ref, lam_ref, h0_ref,
                   gate_ref, u_ref, hf_ref, st_ref, a_s, b_s, state_s, *, tt, d, heads):
    t = pl.program_id(1)
    nt = pl.num_programs(1)
    n = tt * SUB

    @pl.when(t == 0)
    def _():
        state_s[...] = h0_ref[...]

    ng = ng_ref[...]
    sc = sc_ref[0]
    sh = sh_ref[0]

    def nm(rows):
        return _norm_mod(rows.reshape(-1, SUB, d), ng, sc, sh).reshape(rows.shape)

    xm = nm(_to_time_major(h_ref[...]))
    xp = jnp.where(t > 0, nm(_to_time_major(hp_ref[...])[(HALO - 1) * SUB:, :]), 0.0)
    xn = jnp.where(t < nt - 1, nm(_to_time_major(hn_ref[...])[:3 * SUB, :]), 0.0)
    xe = jnp.concatenate([xp, xm, xn], axis=0).astype(BF16)

    xmb = xm.astype(BF16)
    cw = cw_ref[...]
    cb = cb_ref[...]
    hd = d // heads
    for k in range(heads):
        cs = slice(k * hd, (k + 1) * hd)
        gate_ref[0, :, cs] = jnp.dot(xmb, win_ref[:, cs], preferred_element_type=F32).astype(gate_ref.dtype)
        ur = jnp.dot(xe, win_ref[:, d + k * hd:d + (k + 1) * hd], preferred_element_type=F32)
        u = cb[:, cs]
        for i in range(cw.shape[0]):
            u = u + cw[i:i + 1, cs] * ur[i * SUB:i * SUB + n]
        u_ref[0, :, cs] = u
        a, b_in = _rg_gates(u, k, wa_ref, ba_ref, wi_ref, bi_ref, lam_ref)
        a_s[:, cs] = a
        b_s[:, cs] = b_in
    st_ref[...] = _scan_tile(a_s, b_s, state_s, tt, reverse=False)
    hf_ref[0] = b_s[...]


def _rg_bwd_kernel(u_ref, gate_ref, hf_ref, h_ref, g1_ref, wa_ref, ba_ref, wi_ref, bi_ref, lam_ref, h0_ref,
                   wout_ref, o_ref, st_ref, a_s, b_s, state_s, *, tt, d, heads):
    t = pl.program_id(1)

    @pl.when(t == 0)
    def _():
        state_s[...] = h0_ref[...]

    hd = d // heads
    for k in range(heads):
        cs = slice(k * hd, (k + 1) * hd)
        a, b_in = _rg_gates(u_ref[0, :, cs], k, wa_ref, ba_ref, wi_ref, bi_ref, lam_ref)
        a_s[:, cs] = a
        b_s[:, cs] = b_in
    st_ref[...] = _scan_tile(a_s, b_s, state_s, tt, reverse=True)

    y = jnp.concatenate(
        [((hf_ref[0, :, cs] + b_s[:, cs]) * _gelu_tanh(gate_ref[0, :, cs].astype(F32))).astype(BF16)
         for cs in (slice(k * hd, (k + 1) * hd) for k in range(heads))], axis=1)
    out = jnp.dot(y, wout_ref[...], preferred_element_type=F32)
    out = (out.reshape(tt, SUB, d) * g1_ref[0]).reshape(tt * SUB, d)
    o_ref[...] = h_ref[...] + _to_batch_major(out, tt)


def _rg_layer(h, sh, sc, g1, ng, win, cw, cb, wa, ba, wi, bi, lam, wout, h0_f, h0_b):
    b, l, d = h.shape
    heads = wa.shape[1]
    assert b % SUB == 0 and l % HALO == 0, "the recurrent kernels scan SUB batch rows per sublane tile"
    groups = b // SUB
    tt = min(RG_TT, l)
    nt = l // tt
    r = tt // HALO
    n = tt * SUB
    grid = (groups, nt)
    sem = _params(("parallel", "arbitrary"))
    sh, sc, g1 = (v.reshape(groups, SUB, d) for v in (sh, sc, g1))

    vec = pl.BlockSpec((1, SUB, d), lambda gi, ti: (gi, 0, 0))
    st_spec = pl.BlockSpec((SUB, d), lambda gi, ti: (gi, 0))
    tile_f = pl.BlockSpec((SUB, tt, d), lambda gi, ti: (gi, ti, 0))
    tm_f = pl.BlockSpec((1, n, d), lambda gi, ti: (gi, ti, 0))
    prev_spec = pl.BlockSpec((SUB, HALO, d), lambda gi, ti: (gi, jnp.maximum(ti * r - 1, 0), 0))
    next_spec = pl.BlockSpec((SUB, HALO, d), lambda gi, ti: (gi, jnp.minimum((ti + 1) * r, nt * r - 1), 0))
    scratch = [pltpu.VMEM((n, d), F32), pltpu.VMEM((n, d), F32), pltpu.VMEM((SUB, d), F32)]
    tm_shape = (groups, l * SUB, d)

    def dirw(k):
        return wa[k], ba[k], wi[k], bi[k], lam[k]

    fw = dirw(0)
    gate, u, hf, st_f = pl.pallas_call(
        functools.partial(_rg_fwd_kernel, tt=tt, d=d, heads=heads),
        grid=grid,
        in_specs=[prev_spec, tile_f, next_spec, vec, vec, _full(ng), _full(win), _full(cw), _full(cb),
                  *[_full(w) for w in fw], st_spec],
        out_specs=[tm_f, tm_f, tm_f, st_spec],
        out_shape=[jax.ShapeDtypeStruct(tm_shape, BF16), jax.ShapeDtypeStruct(tm_shape, F32),
                   jax.ShapeDtypeStruct(tm_shape, F32), jax.ShapeDtypeStruct((b, d), F32)],
        scratch_shapes=scratch,
        compiler_params=sem,
        name="rg_fwd",
    )(h, h, h, sh, sc, ng, win, cw, cb, *fw, h0_f)

    tile_r = pl.BlockSpec((SUB, tt, d), lambda gi, ti: (gi, nt - 1 - ti, 0))
    tm_r = pl.BlockSpec((1, n, d), lambda gi, ti: (gi, nt - 1 - ti, 0))
    bw = dirw(1)
    h_new, st_b = pl.pallas_call(
        functools.partial(_rg_bwd_kernel, tt=tt, d=d, heads=heads),
        grid=grid,
        in_specs=[tm_r, tm_r, tm_r, tile_r, vec, *[_full(w) for w in bw], st_spec, _full(wout)],
        out_specs=[tile_r, st_spec],
        out_shape=[jax.ShapeDtypeStruct((b, l, d), F32), jax.ShapeDtypeStruct((b, d), F32)],
        scratch_shapes=scratch,
        compiler_params=sem,
        name="rg_bwd",
    )(u, gate, hf, h, g1, *bw, h0_b, wout)
    return h_new, st_f, st_b


def _hy_in_kernel(hp_ref, h_ref, hn_ref, sh_ref, sc_ref, ng_ref, win_ref, cw_ref, cb_ref,
                  x0_ref, xv_ref, *, t_rows, d, ck):
    _, xe = _ext_rows(hp_ref, h_ref, hn_ref, ng_ref[...], sc_ref[0], sh_ref[0])
    offs = (-1, 0, 1)

    def proj(lo):
        z = jnp.dot(xe, win_ref[:, lo:lo + ck], preferred_element_type=F32)
        return _conv_rows(z, cw_ref[:, lo:lo + ck], cb_ref[:, lo:lo + ck], offs, t_rows)

    for ci in range(d // ck):
        lo = ci * ck
        x0_ref[0, :, lo:lo + ck] = proj(lo).astype(BF16)
        xv_ref[0, :, lo:lo + ck] = (proj(d + lo) * proj(2 * d + lo)).astype(BF16)


def _hy_in_call(h, sh, sc, ng, win, cw, cb):
    b, l, d = h.shape
    t_rows = min(2 * ROW_TILE, l)
    ck = _pick(d, COL_CHUNK, LANE)
    vec = pl.BlockSpec((1, 1, d), lambda bi, ti: (bi, 0, 0))
    tile = pl.BlockSpec((1, t_rows, d), lambda bi, ti: (bi, ti, 0))
    return pl.pallas_call(
        functools.partial(_hy_in_kernel, t_rows=t_rows, d=d, ck=ck),
        grid=(b, l // t_rows),
        in_specs=[*_halo_specs(t_rows, l, d), vec, vec, _full(ng), _full(win), _full(cw), _full(cb)],
        out_specs=[tile, tile],
        out_shape=[jax.ShapeDtypeStruct((b, l, d), BF16), jax.ShapeDtypeStruct((b, l, d), BF16)],
        compiler_params=_params(("parallel", "arbitrary")),
        name="hy_in",
    )(h, h, h, sh, sc, ng, win, cw, cb)


def _hy_filter_kernel(w1_ref, b1_ref, w2_ref, b2_ref, w3_ref, b3_ref, w4_ref, fr_ref, k_ref, hdn_s,
                      *, n, d, dc, bands):
    j = pl.program_id(0)
    hp = lax.Precision.HIGHEST

    @pl.when(j == 0)
    def _():
        s = lax.broadcasted_iota(jnp.int32, (n, LANE), 0).astype(F32)
        lane = lax.broadcasted_iota(jnp.int32, (n, LANE), 1)
        tpos = s / float(n - 1)
        w = s * (2.0 * math.pi / n)
        bidx = jnp.where(lane <= bands, lane - 1, lane - 1 - bands).astype(F32)
        band = 1e-4 + bidx * ((bands - 1 - 1e-4) / (bands - 1))
        arg = band * w
        z = jnp.where(lane == 0, tpos,
                      jnp.where(lane <= bands, jnp.cos(arg),
                                jnp.where(lane <= 2 * bands, -jnp.sin(arg), 0.0)))
        fr = fr_ref[...]
        hd = jnp.sin(fr * (jnp.dot(z, w1_ref[...], precision=hp, preferred_element_type=F32) + b1_ref[...]))
        hd = jnp.sin(fr * (jnp.dot(hd, w2_ref[...], precision=hp, preferred_element_type=F32) + b2_ref[...]))
        hd = jnp.sin(fr * (jnp.dot(hd, w3_ref[...], precision=hp, preferred_element_type=F32) + b3_ref[...]))
        hdn_s[...] = hd

    k = jnp.dot(hdn_s[...], w4_ref[...], precision=hp, preferred_element_type=F32)
    srow = lax.broadcasted_iota(jnp.int32, (n, 1), 0).astype(F32)
    centre = n // 2
    dist = jnp.abs(srow - float(centre)) / float(centre)
    ch = (j * dc + lax.broadcasted_iota(jnp.int32, (1, dc), 1)).astype(F32)
    d_lo = math.log(HY_TARGET) / HY_SLOW_DECAY
    d_hi = math.log(HY_TARGET) / HY_FAST_DECAY
    delta = d_lo + ch * ((d_hi - d_lo) / (d - 1))
    k = k * jnp.exp(-dist * jnp.abs(delta))
    k = k / jnp.sum(jnp.abs(k), axis=0, keepdims=True)
    k_ref[...] = k.astype(k_ref.dtype)


def _hy_filter_call(n, w1, b1, w2, b2, w3, b3, w4, fr):
    emb, fd = w1.shape
    d = w4.shape[1]
    bands = (emb - 1) // 2
    w1p = jnp.zeros((LANE, fd), F32).at[:emb].set(w1)
    dc = _pick(d, COL_CHUNK, LANE)
    args = (w1p, b1.reshape(1, fd), w2, b2.reshape(1, fd), w3, b3.reshape(1, fd))
    return pl.pallas_call(
        functools.partial(_hy_filter_kernel, n=n, d=d, dc=dc, bands=bands),
        grid=(d // dc,),
        in_specs=[*[_full(a, single=False) for a in args],
                  pl.BlockSpec((fd, dc), lambda j: (0, j)), _full(fr.reshape(1, fd), single=False)],
        out_specs=pl.BlockSpec((n, dc), lambda j: (0, j)),
        out_shape=jax.ShapeDtypeStruct((n, d), BF16),
        scratch_shapes=[pltpu.VMEM((n, fd), F32)],
        compiler_params=_params(("arbitrary",)),
        name="hy_filter",
    )(*args, w4, fr.reshape(1, fd))


def _dft_mats(n):
    i = 2 * jnp.arange(n, dtype=jnp.int32) + 1
    m = (i[:, None] * i[None, :]) % (8 * n)
    ang = m.astype(F32) * (2.0 * math.pi / (8 * n))
    return jnp.cos(ang).astype(BF16), jnp.sin(ang).astype(BF16)


def _hy_spec_kernel(c_ref, s_ref, k_ref, kr_ref, ki_ref, *, n, fb):
    j = pl.program_id(0)
    kk = k_ref[...]
    a = jnp.dot(c_ref[...], kk, preferred_element_type=F32)
    b = jnp.dot(s_ref[...], kk, preferred_element_type=F32)
    fi = 2 * (j * fb + lax.broadcasted_iota(jnp.int32, (fb, 1), 0)) + 1
    m = lax.rem(fi * (n + 1), 8 * n)
    ang = m.astype(F32) * (2.0 * math.pi / (8 * n))
    qr = jnp.cos(ang) * (1.0 / n)
    qi = jnp.sin(ang) * (1.0 / n)
    kr_ref[...] = qr * a + qi * b
    ki_ref[...] = qi * a - qr * b


def _hy_spec_call(cm, sm, k):
    n, d = k.shape
    fb = min(DFT_FB, n)
    row = pl.BlockSpec((fb, n), lambda j: (j, 0))
    out = pl.BlockSpec((fb, d), lambda j: (j, 0))
    return pl.pallas_call(
        functools.partial(_hy_spec_kernel, n=n, fb=fb),
        grid=(n // fb,),
        in_specs=[row, row, _full(k)],
        out_specs=[out, out],
        out_shape=[jax.ShapeDtypeStruct((n, d), F32), jax.ShapeDtypeStruct((n, d), F32)],
        compiler_params=_params(("arbitrary",)),
        name="hy_spec_dense",
    )(cm, sm, k)


def _hy_conv_kernel(c_ref, s_ref, xv_ref, kr_ref, ki_ref, x0_ref, h_ref, skip_ref, g1_ref, wout_ref,
                    o_ref, zr_s, zi_s, *, fb):
    p = pl.program_id(1)
    j = pl.program_id(2)
    rows = pl.ds(pl.multiple_of(j * fb, fb), fb)

    @pl.when(p == 0)
    def _():
        xv = xv_ref[0]
        xr = jnp.dot(c_ref[...], xv, preferred_element_type=F32)
        xi = -jnp.dot(s_ref[...], xv, preferred_element_type=F32)
        kr = kr_ref[...]
        ki = ki_ref[...]
        zr_s[rows, :] = (xr * kr - xi * ki).astype(BF16)
        zi_s[rows, :] = (xr * ki + xi * kr).astype(BF16)

    @pl.when(p == 1)
    def _():
        y = (jnp.dot(c_ref[...], zr_s[...], preferred_element_type=F32)
             - jnp.dot(s_ref[...], zi_s[...], preferred_element_type=F32))
        y = y + xv_ref[0, rows, :].astype(F32) * skip_ref[...]
        pv = (x0_ref[0].astype(F32) * y).astype(BF16)
        out = jnp.dot(pv, wout_ref[...], preferred_element_type=F32)
        o_ref[0] = h_ref[0] + g1_ref[0] * out


def _hy_conv_call(cm, sm, xv, kr, ki, x0, h, skip, g1, wout):
    b, n, d = h.shape
    fb = min(DFT_FB, n)
    nf = n // fb
    row = pl.BlockSpec((fb, n), lambda bi, p, j: (j, 0))
    kspec = pl.BlockSpec((fb, d), lambda bi, p, j: (jnp.where(p == 0, j, nf - 1), 0))
    tile = pl.BlockSpec((1, fb, d), lambda bi, p, j: (bi, jnp.where(p == 1, j, 0), 0))
    return pl.pallas_call(
        functools.partial(_hy_conv_kernel, fb=fb),
        grid=(b, 2, nf),
        in_specs=[row, row,
                  pl.BlockSpec((1, n, d), lambda bi, p, j: (bi, 0, 0), pipeline_mode=pl.Buffered(1)),
                  kspec, kspec, tile, tile, _full(skip),
                  pl.BlockSpec((1, 1, d), lambda bi, p, j: (bi, 0, 0)), _full(wout)],
        out_specs=tile,
        out_shape=jax.ShapeDtypeStruct((b, n, d), F32),
        scratch_shapes=[pltpu.VMEM((n, d), BF16), pltpu.VMEM((n, d), BF16)],
        compiler_params=_params(("parallel", "arbitrary", "arbitrary")),
        name="hy_conv_dense",
    )(cm, sm, xv, kr, ki, x0, h, skip, g1, wout)


def _fft_mats(n):
    p, r = FFT_P, FFT_R
    nn = 2 * n
    q, qh, hh = nn // p, n // p, p // r
    i32 = jnp.int32
    t1h = jnp.arange(hh, dtype=i32).reshape(hh, 1, 1, 1)
    f2 = jnp.arange(q, dtype=i32).reshape(1, q, 1, 1)
    t1l = jnp.arange(r, dtype=i32).reshape(1, 1, r, 1)
    t2 = jnp.arange(qh, dtype=i32).reshape(1, 1, 1, qh)
    m = ((2 * f2 + 1) * (p * t2 + r * t1h + t1l)) % (2 * nn)
    th = m.astype(F32) * (math.pi / nn)
    base = jnp.stack([jnp.cos(th), -jnp.sin(th)], axis=1).astype(BF16)
    eye = jnp.eye(r, dtype=BF16)
    w1 = base[..., None] * eye[None, None, None, :, None, :]
    w1f = w1.reshape(hh, 2 * q * r, qh * r)
    w1i = jnp.transpose(w1, (0, 4, 5, 1, 2, 3)).reshape(hh, qh * r, 2 * q * r)
    f1 = jnp.arange(p // 2, dtype=i32)[:, None]
    t1 = jnp.arange(p, dtype=i32)[None, :]
    phi = ((f1 * t1) % p).astype(F32) * (2.0 * math.pi / p)
    c, sn = jnp.cos(phi), jnp.sin(phi)
    w2 = jnp.block([[c, sn], [-sn, c]]).astype(BF16)
    w2i = jnp.block([[c.T, -sn.T], [sn.T, c.T]]).astype(BF16)
    return w1f, w1i, w2, w2i


def _fft_coarse_fwd(load_rows, w1_ref, s_ref, n):
    p, r = FFT_P, FFT_R
    q, qh, hh = 2 * n // p, n // p, p // r

    def body(t1h, carry):
        u = jnp.concatenate([load_rows(pl.multiple_of(p * t2 + t1h * r, r)) for t2 in range(qh)], axis=0)
        res = jnp.dot(w1_ref[t1h], u, preferred_element_type=F32)
        for pf in range(2 * q):
            s_ref[pf * hh + t1h] = res[pf * r:(pf + 1) * r].astype(BF16)
        return carry

    lax.fori_loop(0, hh, body, 0, unroll=FFT_UNROLL)


def _fft_fine_fwd(s_ref, w2_ref, f2, n):
    p, r = FFT_P, FFT_R
    q, hh = 2 * n // p, p // r
    cc = s_ref.shape[-1]
    v = jnp.concatenate([s_ref[pl.ds(f2 * hh, hh)].reshape(p, cc),
                         s_ref[pl.ds((q + f2) * hh, hh)].reshape(p, cc)], axis=0)
    return jnp.dot(w2_ref[...], v, preferred_element_type=F32)


def _hy_kspec2_kernel(w1_ref, w2_ref, k_ref, o_ref, s_ref, *, n):
    p = FFT_P
    q = 2 * n // p
    _fft_coarse_fwd(lambda r0: k_ref[pl.ds(r0, FFT_R), :], w1_ref, s_ref, n)

    def body(f2, carry):
        xs = _fft_fine_fwd(s_ref, w2_ref, f2, n)
        xr, xi = xs[:p // 2], xs[p // 2:]
        f1 = lax.broadcasted_iota(jnp.int32, (p // 2, 1), 0)
        m = lax.rem(2 * (q * f1 + f2) + 1, 8)
        ang = m.astype(F32) * (math.pi / 4.0)
        qr = jnp.cos(ang) * (1.0 / n)
        qi = jnp.sin(ang) * (1.0 / n)
        o_ref[f2, 0] = qr * xr - qi * xi
        o_ref[f2, 1] = qr * xi + qi * xr
        return carry

    lax.fori_loop(0, q, body, 0)


def _hy_kspec2_call(w1, w2, k):
    n, d = k.shape
    p, r = FFT_P, FFT_R
    q, hh = 2 * n // p, p // r
    cc = _pick(d, COL_CHUNK, LANE)
    return pl.pallas_call(
        functools.partial(_hy_kspec2_kernel, n=n),
        grid=(d // cc,),
        in_specs=[_full(w1), _full(w2), pl.BlockSpec((n, cc), lambda c: (0, c))],
        out_specs=pl.BlockSpec((q, 2, p // 2, cc), lambda c: (0, 0, 0, c)),
        out_shape=jax.ShapeDtypeStruct((q, 2, p // 2, d), F32),
        scratch_shapes=[pltpu.VMEM((2 * q * hh, r, cc), BF16)],
        compiler_params=_params(("arbitrary",)),
        name="hy_kspec",
    )(w1, w2, k)


def _hy_conv2_kernel(w1_ref, w1i_ref, w2_ref, w2i_ref, xv_ref, x0_ref, k_ref, skip_ref, p_ref, s_ref, *, n):
    p, r = FFT_P, FFT_R
    q, qh, hh = 2 * n // p, n // p, p // r
    _fft_coarse_fwd(lambda r0: xv_ref[0, pl.ds(r0, r), :], w1_ref, s_ref, n)

    def mid(f2, carry):
        xs = _fft_fine_fwd(s_ref, w2_ref, f2, n)
        xr, xi = xs[:p // 2], xs[p // 2:]
        kr = k_ref[f2, 0]
        ki = k_ref[f2, 1]
        z = jnp.concatenate([xr * kr - xi * ki, xr * ki + xi * kr], axis=0).astype(BF16)
        c = jnp.dot(w2i_ref[...], z, preferred_element_type=F32)
        cc = c.shape[-1]
        s_ref[pl.ds(f2 * hh, hh)] = c[:p].reshape(hh, r, cc).astype(BF16)
        s_ref[pl.ds((q + f2) * hh, hh)] = c[p:].reshape(hh, r, cc).astype(BF16)
        return carry

    lax.fori_loop(0, q, mid, 0, unroll=FFT_UNROLL)
    skip = skip_ref[...]

    def back(t1h, carry):
        g = jnp.concatenate([s_ref[pf * hh + t1h] for pf in range(2 * q)], axis=0)
        y = jnp.dot(w1i_ref[t1h], g, preferred_element_type=F32)
        for t2 in range(qh):
            rows = pl.ds(pl.multiple_of(p * t2 + t1h * r, r), r)
            yv = y[t2 * r:(t2 + 1) * r] + xv_ref[0, rows, :].astype(F32) * skip
            p_ref[0, rows, :] = (x0_ref[0, rows, :].astype(F32) * yv).astype(p_ref.dtype)
        return carry

    lax.fori_loop(0, hh, back, 0, unroll=FFT_UNROLL)


def _hy_conv2_call(mats, xv, x0, kspec, skip):
    b, n, d = xv.shape
    p, r = FFT_P, FFT_R
    q, hh = 2 * n // p, p // r
    cc = _pick(d, COL_CHUNK, LANE)
    w1, w1i, w2, w2i = mats
    col = pl.BlockSpec((1, n, cc), lambda c, bi: (bi, 0, c))
    return pl.pallas_call(
        functools.partial(_hy_conv2_kernel, n=n),
        grid=(d // cc, b),
        in_specs=[_full(w1), _full(w1i), _full(w2), _full(w2i), col, col,
                  pl.BlockSpec((q, 2, p // 2, cc), lambda c, bi: (0, 0, 0, c), pipeline_mode=pl.Buffered(1)),
                  pl.BlockSpec((1, cc), lambda c, bi: (0, c))],
        out_specs=col,
        out_shape=jax.ShapeDtypeStruct((b, n, d), BF16),
        scratch_shapes=[pltpu.VMEM((2 * q * hh, r, cc), BF16)],
        compiler_params=_params(("arbitrary", "arbitrary")),
        name="hy_conv2",
    )(w1, w1i, w2, w2i, xv, x0, kspec, skip)


def _hy_out_kernel(p_ref, h_ref, g1_ref, wout_ref, o_ref):
    o_ref[0] = h_ref[0] + g1_ref[0] * jnp.dot(p_ref[0], wout_ref[...], preferred_element_type=F32)


def _hy_out_call(pv, h, g1, wout):
    b, l, d = h.shape
    t_rows = min(2 * ROW_TILE, l)
    tile = pl.BlockSpec((1, t_rows, d), lambda bi, ti: (bi, ti, 0))
    return pl.pallas_call(
        _hy_out_kernel,
        grid=(b, l // t_rows),
        in_specs=[tile, tile, pl.BlockSpec((1, 1, d), lambda bi, ti: (bi, 0, 0)), _full(wout)],
        out_specs=tile,
        out_shape=jax.ShapeDtypeStruct((b, l, d), F32),
        compiler_params=_params(("parallel", "arbitrary")),
        name="hy_out",
    )(pv, h, g1, wout)


def _to_col_kernel(x_ref, o_ref, *, rows, d):
    x4 = x_ref[0].reshape(rows // SUB, SUB, SUB, d)
    res = _swap_slab_sublane([x4[:, k] for k in range(SUB)])
    for w in range(SUB):
        o_ref[0, w * rows:(w + 1) * rows, :] = res[w].reshape(rows, d)


def _to_col_major(x):
    b, l, d = x.shape
    rows = l // GRID_W
    assert rows % SUB == 0
    return pl.pallas_call(
        functools.partial(_to_col_kernel, rows=rows, d=d),
        grid=(b, GRID_W // SUB),
        in_specs=[pl.BlockSpec((1, rows, SUB, d), lambda bi, wi: (bi, 0, wi, 0))],
        out_specs=pl.BlockSpec((1, SUB * rows, d), lambda bi, wi: (bi, wi, 0)),
        out_shape=jax.ShapeDtypeStruct((b, l, d), x.dtype),
        compiler_params=_params(("parallel", "arbitrary")),
        name="to_col_major",
    )(x.reshape(b, rows, GRID_W, d))


def _final_col_kernel(x_ref, g_ref, o_ref, *, rows, d):
    g = g_ref[...]
    slabs = [_rms(x_ref[0, w * rows:(w + 1) * rows, :], g).reshape(rows // SUB, SUB, d) for w in range(SUB)]
    res = _swap_slab_sublane(slabs)
    o_ref[0] = jnp.stack(res, axis=1).reshape(rows, SUB, d)


def _final_row_kernel(x_ref, g_ref, o_ref):
    o_ref[0] = _rms(x_ref[0], g_ref[...])


def _final_norm(x, g, col_major):
    b, l, d = x.shape
    g2 = g.reshape(1, d)
    if not col_major:
        t_rows = min(ROW_TILE, l)
        tile = pl.BlockSpec((1, t_rows, d), lambda bi, ti: (bi, ti, 0))
        return pl.pallas_call(
            _final_row_kernel, grid=(b, l // t_rows), in_specs=[tile, _full(g2)], out_specs=tile,
            out_shape=jax.ShapeDtypeStruct((b, l, d), F32),
            compiler_params=_params(("parallel", "arbitrary")),
            name="final_norm",
        )(x, g2)
    rows = l // GRID_W
    assert rows % SUB == 0
    out = pl.pallas_call(
        functools.partial(_final_col_kernel, rows=rows, d=d),
        grid=(b, GRID_W // SUB),
        in_specs=[pl.BlockSpec((1, SUB * rows, d), lambda bi, wi: (bi, wi, 0)), _full(g2)],
        out_specs=pl.BlockSpec((1, rows, SUB, d), lambda bi, wi: (bi, 0, wi, 0)),
        out_shape=jax.ShapeDtypeStruct((b, rows, GRID_W, d), F32),
        compiler_params=_params(("parallel", "arbitrary")),
        name="final_norm_col",
    )(x, g2)
    return out.reshape(b, l, d)


def kernel(x, c, ctx, c_ctx, mod_w, mod_b, norm1_g, norm2_g, final_g, rg_w_in, rg_conv_w, rg_conv_b, rg_w_a, rg_b_a, rg_w_i, rg_b_i, rg_lam, rg_w_out, hy_w_in, hy_short_w, hy_short_b, hy_pe_w1, hy_pe_b1, hy_pe_w2, hy_pe_b2, hy_pe_w3, hy_pe_b3, hy_pe_w4, hy_freq, hy_skip, hy_w_out, ffn_w_up, ffn_conv_w, ffn_conv_b, ffn_w_down):
    b, l, d = x.shape
    lc = ctx.shape[1]
    depth = mod_w.shape[0]
    n_mixers = 2
    ctx_needed = [any((q % n_mixers) == 0 for q in range(i + 1, depth)) for i in range(depth)]

    bp = -(-(b + 1) // HALO) * HALO
    cpad = jnp.zeros((bp, d), F32).at[:b].set(c).at[b].set(c_ctx)
    mods = _mod_call(cpad, mod_w, mod_b)

    dft = {}
    h = x
    s = ctx
    col = False
    zeros_state = jnp.zeros((b, d), F32)
    fg = final_g.reshape(1, d)
    for i in range(depth):
        kind = i % n_mixers
        j = i // n_mixers
        col_major = j % 2 == 1
        keep_ctx = ctx_needed[i]
        if col_major != col:
            assert col_major, "column-major layers are expected to be contiguous at the end"
            h = _to_col_major(h)
            col = True
        lat = [mods[i, :b, q * d:(q + 1) * d].reshape(b, 1, d) for q in range(6)]
        cmod = [jnp.broadcast_to(mods[i, b, q * d:(q + 1) * d].reshape(1, 1, d), (b, 1, d)) for q in range(6)]
        ng1 = norm1_g[i].reshape(1, d)
        ng2 = norm2_g[i].reshape(1, d)

        if kind == 0:
            w = (ng1, rg_w_in[j].astype(BF16), rg_conv_w[j], rg_conv_b[j].reshape(1, d),
                 (0.5 * rg_w_a[j]).astype(BF16), 0.5 * rg_b_a[j].reshape(2, 1, d), (0.5 * rg_w_i[j]).astype(BF16),
                 0.5 * rg_b_i[j].reshape(2, 1, d), rg_lam[j].reshape(2, 1, d), rg_w_out[j].astype(BF16))
            s_new, st_f, st_b = _rg_layer(s, cmod[0], cmod[1], cmod[2], *w, zeros_state, zeros_state)
            h, _, _ = _rg_layer(h, lat[0], lat[1], lat[2], *w, st_f, st_b)
            if keep_ctx:
                s = s_new
        else:
            win = hy_w_in[j].astype(BF16)
            cw = hy_short_w[j]
            cb = hy_short_b[j].reshape(1, 3 * d)
            wout = hy_w_out[j].astype(BF16)
            skip = hy_skip[j].reshape(1, d)
            pe = (hy_pe_w1[j], hy_pe_b1[j], hy_pe_w2[j], hy_pe_b2[j], hy_pe_w3[j], hy_pe_b3[j], hy_pe_w4[j],
                  hy_freq[j])
            streams = [(h, lat, True)] + ([(s, cmod, False)] if keep_ctx else [])
            for stream, md, is_lat in streams:
                n = stream.shape[1]
                two_level = n % FFT_P == 0 and n // FFT_P >= 2
                if n not in dft:
                    dft[n] = _fft_mats(n) if two_level else _dft_mats(n)
                kf = _hy_filter_call(n, *pe)
                x0, xv = _hy_in_call(stream, md[0], md[1], ng1, win, cw, cb)
                if two_level:
                    w1, w1i, w2, w2i = dft[n]
                    pv = _hy_conv2_call(dft[n], xv, x0, _hy_kspec2_call(w1, w2, kf), skip)
                    new = _hy_out_call(pv, stream, md[2], wout)
                else:
                    cm, sm = dft[n]
                    kr, ki = _hy_spec_call(cm, sm, kf)
                    new = _hy_conv_call(cm, sm, xv, kr, ki, x0, stream, skip, md[2], wout)
                if is_lat:
                    h = new
                else:
                    s = new

        ffn = (ng2, ffn_w_up[i].astype(BF16), ffn_conv_w[i], ffn_conv_b[i].reshape(1, -1), ffn_w_down[i].astype(BF16))
        fuse_col, fuse_final = _ffn_fusable(l)
        next_col = i + 1 < depth and ((i + 1) // n_mixers) % 2 == 1
        mode = "plain"
        if i + 1 == depth and col and fuse_final:
            mode = "final_raster"
        elif next_col and not col and fuse_col:
            mode = "col_major"
        h = _ffn_call(h, lat[3], lat[4], lat[5], *ffn, fg, out_mode=mode)
        if mode == "final_raster":
            return h
        col = col or mode == "col_major"
        if keep_ctx:
            s = _ffn_call(s, cmod[3], cmod[4], cmod[5], *ffn, fg)
    return _final_norm(h, final_g, col)
```
